```python
import math
import jax, jax.numpy as jnp
from jax import lax
import numpy as np

D_MODEL = 4096
BATCH = 4
SEQ = 4096
DEPTH = 1

CHUNK = 64
Q_BLOCK = 128
N_MEM = 256
EPS = 1e-6

H_A = 16
DH_A = 128
DV_A = 128
Q_LORA = 1536
KV_LORA = 512
H_IDX = 16
D_IDX = 128
TOPK_MAX = 256

H_B = 16
DK_B = 128
DV_B = 128
CONV_WIDTH = 4

D_MIX = H_A * DV_A + H_B * DV_B
IN_SIZES = (Q_LORA, KV_LORA, D_IDX, H_IDX,
            2 * H_B * DK_B + H_B * DV_B, H_B * DV_B,
            H_B, H_B)
N_IN = sum(IN_SIZES)

H_X = 4
DH_X = 128

D_FF = ((8 * D_MODEL + 3 * 256 - 1) // (3 * 256)) * 256

kernel_name = "hybrid_dsa_gdn_stream_block"


def rmsnorm(x, g):
    xf = x.astype(jnp.float32)
    y = xf * lax.rsqrt(jnp.mean(xf * xf, axis=-1, keepdims=True) + EPS)
    return (y * g.astype(jnp.float32)).astype(x.dtype)


def l2norm(x):
    xf = x.astype(jnp.float32)
    return xf * lax.rsqrt(jnp.sum(xf * xf, axis=-1, keepdims=True) + EPS)


def causal_depthwise_conv(x, w):
    width, ch = w.shape
    return lax.conv_general_dilated(x, w[:, None, :], window_strides=(1,), padding=[(width - 1, 0)],
                                    dimension_numbers=('NWC', 'WIO', 'NWC'), feature_group_count=ch)


def sparse_mla_attention(qa, ckv, k_idx, w_idx, g_qa, w_qb, g_kv, w_uk, w_uv, w_iq):
    B, S, _ = qa.shape
    topk = min(TOPK_MAX, S // 4)
    qa = rmsnorm(qa, g_qa)
    ckv = rmsnorm(ckv, g_kv)
    q = (qa @ w_qb).reshape(B, S, H_A, DH_A)
    q_idx = (qa @ w_iq).reshape(B, S, H_IDX, D_IDX)
    w_idx = w_idx * (H_IDX ** -0.5)
    key_chunk = jnp.arange(S) // CHUNK
    nb = S // Q_BLOCK
    gather = jax.vmap(lambda table, idx: table[idx])

    def blockify(t):
        return t.reshape(B, nb, Q_BLOCK, *t.shape[2:]).swapaxes(0, 1)

    def one_block(args):
        q_b, q_idx_b, w_b, q_pos = args
        q_chunk = q_pos // CHUNK
        admissible = key_chunk[None, :] <= q_chunk[:, None]
        rel = jax.nn.relu(jnp.einsum('bqhd,bsd->bqhs', q_idx_b, k_idx) * (D_IDX ** -0.5))
        score = jnp.einsum('bqhs,bqh->bqs', rel, w_b).astype(jnp.float32)
        score = jnp.where(admissible[None], score, -jnp.inf)
        _, idx = lax.top_k(score, topk)
        valid = key_chunk[idx] <= q_chunk[None, :, None]
        c_sel = gather(ckv, idx)
        q_lat = jnp.einsum('bqhd,hcd->bqhc', q_b, w_uk)
        logits = jnp.einsum('bqhc,bqkc->bqhk', q_lat, c_sel).astype(jnp.float32) * (DH_A ** -0.5)
        logits = jnp.where(valid[:, :, None, :], logits, -jnp.inf)
        p = jax.nn.softmax(logits, axis=-1).astype(ckv.dtype)
        o_lat = jnp.einsum('bqhk,bqkc->bqhc', p, c_sel)
        return jnp.einsum('bqhc,hce->bqhe', o_lat, w_uv)

    pos = jnp.arange(S).reshape(nb, Q_BLOCK)
    out = lax.map(one_block, (blockify(q), blockify(q_idx), blockify(w_idx), pos))
    return out.swapaxes(0, 1).reshape(B, S, H_A * DV_A)


def chunk_gated_delta_rule(q, k, v, g, beta):
    B, S, H, DK = q.shape
    DV = v.shape[-1]
    N = S // CHUNK

    def to_chunks(t):
        return jnp.moveaxis(t.reshape(B, N, CHUNK, H, *t.shape[3:]), 3, 1)

    q, k, v, g, beta = (to_chunks(t) for t in (q, k, v, g, beta))
    G = jnp.cumsum(g, axis=-1)
    incl = jnp.tril(jnp.ones((CHUNK, CHUNK), bool))
    strict = jnp.tril(jnp.ones((CHUNK, CHUNK), bool), -1)
    decay = jnp.exp(jnp.where(incl, G[..., :, None] - G[..., None, :], -jnp.inf))
    kb = k * beta[..., None]
    lower = jnp.where(strict, jnp.einsum('bhnid,bhnjd->bhnij', kb, k) * decay, 0.0)
    tmat = lower + jnp.eye(CHUNK, dtype=lower.dtype)
    u = lax.linalg.triangular_solve(tmat, v * beta[..., None], left_side=True, lower=True, unit_diagonal=True)
    w = lax.linalg.triangular_solve(tmat, kb * jnp.exp(G)[..., None], left_side=True, lower=True, unit_diagonal=True)
    intra = jnp.where(incl, jnp.einsum('bhnid,bhnjd->bhnij', q, k) * decay, 0.0)
    q_dec = q * jnp.exp(G)[..., None]
    k_dec = k * jnp.exp(G[..., -1:] - G)[..., None]
    chunk_decay = jnp.exp(G[..., -1])

    def step(state, xs):
        q_i, k_i, u_i, w_i, a_i, d_i = xs
        v_new = u_i - jnp.einsum('bhcd,bhde->bhce', w_i, state)
        o_i = jnp.einsum('bhcd,bhde->bhce', q_i, state) + jnp.einsum('bhij,bhje->bhie', a_i, v_new)
        state = state * d_i[..., None, None] + jnp.einsum('bhcd,bhce->bhde', k_i, v_new)
        return state, o_i

    xs = tuple(jnp.moveaxis(t, 2, 0) for t in (q_dec, k_dec, u, w, intra, chunk_decay))
    state0 = jnp.zeros((B, H, DK, DV), jnp.float32)
    _, o = lax.scan(step, state0, xs)
    o = jnp.moveaxis(o, 0, 2)
    return jnp.moveaxis(o, 1, 3).reshape(B, S, H, DV)


def gated_deltanet(qkv, z, b, a, conv_w, a_log, dt_bias, g_out):
    B, S, _ = qkv.shape
    qkv = jax.nn.silu(causal_depthwise_conv(qkv, conv_w))
    q, k, v = jnp.split(qkv, [H_B * DK_B, 2 * H_B * DK_B], axis=-1)
    q = l2norm(q.reshape(B, S, H_B, DK_B)) * (DK_B ** -0.5)
    k = l2norm(k.reshape(B, S, H_B, DK_B))
    v = v.reshape(B, S, H_B, DV_B).astype(jnp.float32)
    beta = jax.nn.sigmoid(b.astype(jnp.float32))
    g = -jnp.exp(a_log.astype(jnp.float32)) * jax.nn.softplus(a.astype(jnp.float32) + dt_bias.astype(jnp.float32))
    o = chunk_gated_delta_rule(q, k, v, g, beta)
    o = rmsnorm(o, g_out) * jax.nn.silu(z.reshape(B, S, H_B, DV_B).astype(jnp.float32))
    return o.reshape(B, S, H_B * DV_B).astype(qkv.dtype)


def memory_cross_attention(h, mem, w_cq, w_ckv, w_co):
    B, S, _ = h.shape
    M = mem.shape[1]
    q = (h @ w_cq).reshape(B, S, H_X, DH_X)
    k, v = jnp.split(mem @ w_ckv, 2, axis=-1)
    k = k.reshape(B, M, H_X, DH_X)
    v = v.reshape(B, M, H_X, DH_X)
    logits = jnp.einsum('bshd,bmhd->bhsm', q, k).astype(jnp.float32) * (DH_X ** -0.5)
    p = jax.nn.softmax(logits, axis=-1).astype(h.dtype)
    o = jnp.einsum('bhsm,bmhd->bshd', p, v).reshape(B, S, H_X * DH_X)
    return o @ w_co


def swiglu(h, w_in, w_out):
    gate, up = jnp.split(h @ w_in, 2, axis=-1)
    return (jax.nn.silu(gate) * up) @ w_out


def setup_inputs(seed: int = 0) -> dict:
    key = jax.random.key(seed)
    ks = iter(jax.random.split(key, 40))
    f32 = jnp.float32
    L = DEPTH

    def w(shape, fan_in):
        return jax.random.normal(next(ks), shape, f32) * (fan_in ** -0.5)

    def gain(shape):
        return 1.0 + 0.02 * jax.random.normal(next(ks), shape, f32)

    x = jax.random.normal(next(ks), (BATCH, SEQ, D_MODEL), f32)
    mem = jax.random.normal(next(ks), (BATCH, N_MEM, D_MODEL), f32)
    a_log = jnp.log(jax.random.uniform(next(ks), (L, H_B), f32, 1.0, 16.0))
    dt = jnp.exp(jax.random.uniform(next(ks), (L, H_B), f32, math.log(1e-3), math.log(1e-1)))
    dt_bias = dt + jnp.log(-jnp.expm1(-dt))
    return {
        "x": x,
        "mem": mem,
        "attn_norm_g": gain((L, D_MODEL)),
        "w_in": w((L, D_MODEL, N_IN), D_MODEL),
        "qa_norm_g": gain((L, Q_LORA)),
        "w_qb": w((L, Q_LORA, H_A * DH_A), Q_LORA),
        "kv_norm_g": gain((L, KV_LORA)),
        "w_uk": w((L, H_A, KV_LORA, DH_A), KV_LORA),
        "w_uv": w((L, H_A, KV_LORA, DV_A), KV_LORA),
        "w_iq": w((L, Q_LORA, H_IDX * D_IDX), Q_LORA),
        "conv_w": w((L, CONV_WIDTH, 2 * H_B * DK_B + H_B * DV_B), CONV_WIDTH),
        "a_log": a_log,
        "dt_bias": dt_bias,
        "delta_norm_g": gain((L, DV_B)),
        "w_o": w((L, D_MIX, D_MODEL), D_MIX),
        "cross_norm_g": gain((L, D_MODEL)),
        "mem_norm_g": gain((L, D_MODEL)),
        "w_cq": w((L, D_MODEL, H_X * DH_X), D_MODEL),
        "w_ckv": w((L, D_MODEL, 2 * H_X * DH_X), D_MODEL),
        "w_co": w((L, H_X * DH_X, D_MODEL), H_X * DH_X),
        "ffn_norm_g": gain((L, D_MODEL)),
        "w_ffn_in": w((L, D_MODEL, 2 * D_FF), D_MODEL),
        "w_ffn_out": w((L, D_FF, D_MODEL), D_FF),
        "final_norm_g": gain((D_MODEL,)),
    }


def reference(x, mem, attn_norm_g, w_in, qa_norm_g, w_qb, kv_norm_g, w_uk, w_uv, w_iq, conv_w, a_log, dt_bias,
              delta_norm_g, w_o, cross_norm_g, mem_norm_g, w_cq, w_ckv, w_co, ffn_norm_g, w_ffn_in, w_ffn_out,
              final_norm_g):
    split_points = np.cumsum(IN_SIZES)[:-1].tolist()
    h = x
    for l in range(DEPTH):
        n = rmsnorm(h, attn_norm_g[l])
        qa, ckv, k_idx, w_idx, qkv_b, z_b, b_b, a_b = jnp.split(n @ w_in[l], split_points, axis=-1)
        y_a = sparse_mla_attention(qa, ckv, k_idx, w_idx, qa_norm_g[l], w_qb[l], kv_norm_g[l], w_uk[l], w_uv[l], w_iq[l])
        y_b = gated_deltanet(qkv_b, z_b, b_b, a_b, conv_w[l], a_log[l], dt_bias[l], delta_norm_g[l])
        h = h + jnp.concatenate([y_a, y_b], axis=-1) @ w_o[l]
        h = h + memory_cross_attention(rmsnorm(h, cross_norm_g[l]), rmsnorm(mem, mem_norm_g[l]), w_cq[l], w_ckv[l], w_co[l])
        h = h + swiglu(rmsnorm(h, ffn_norm_g[l]), w_ffn_in[l], w_ffn_out[l])
    return rmsnorm(h, final_norm_g)
```

```python
import functools

import jax
import jax.numpy as jnp
from jax import lax
from jax.experimental import pallas as pl
from jax.experimental.pallas import tpu as pltpu

F32 = jnp.float32
BF16 = jnp.bfloat16
I32 = jnp.int32

EPS = 1e-6
CHUNK = 64
Q_BLOCK = 128
TOPK_MAX = 256
LANE = 128
VMEM_LIMIT = 56 * 1024 * 1024
INT_MIN = -2 ** 31
CHUNK_SHIFT = CHUNK.bit_length() - 1
assert 1 << CHUNK_SHIFT == CHUNK
NEG_BIG = -1e30


def _cparams(sem):
    return pltpu.CompilerParams(dimension_semantics=sem, vmem_limit_bytes=VMEM_LIMIT)


def _dot(a, b):
    return jnp.dot(a, b, preferred_element_type=F32)


def _dot_nt(a, b):
    return lax.dot_general(a, b, (((1,), (1,)), ((), ())), preferred_element_type=F32)


def _sigmoid(x):
    return 1.0 / (1.0 + jnp.exp(-x))


def _rmsnorm_kernel(x_ref, g_ref, o_ref):
    x = x_ref[...].astype(F32)
    ms = jnp.mean(x * x, axis=-1, keepdims=True)
    o_ref[...] = (x * lax.rsqrt(ms + EPS) * g_ref[...]).astype(o_ref.dtype)


def rmsnorm_cols(x, g, width, col_block, out_dtype, tr=256):
    m = x.shape[0]
    tr = min(tr, m)
    return pl.pallas_call(
        _rmsnorm_kernel,
        out_shape=jax.ShapeDtypeStruct((m, width), out_dtype),
        grid=(m // tr,),
        in_specs=[pl.BlockSpec((tr, width), lambda i: (i, col_block)),
                  pl.BlockSpec((1, width), lambda i: (0, 0))],
        out_specs=pl.BlockSpec((tr, width), lambda i: (i, 0)),
        compiler_params=_cparams(("parallel",)),
        name="rmsnorm",
    )(x, g.reshape(1, width).astype(F32))


def _mm_kernel(*refs, n_pairs, nk, has_res):
    a_refs = refs[0:2 * n_pairs:2]
    b_refs = refs[1:2 * n_pairs:2]
    pos = 2 * n_pairs
    r_ref = refs[pos] if has_res else None
    pos += int(has_res)
    o_ref = refs[pos]
    acc_ref = refs[pos + 1] if nk > 1 else None

    part = _dot(a_refs[0][...], b_refs[0][...])
    for a_ref, b_ref in zip(a_refs[1:], b_refs[1:]):
        part = part + _dot(a_ref[...], b_ref[...])

    def finish(acc):
        if has_res:
            acc = acc + r_ref[...]
        o_ref[...] = acc.astype(o_ref.dtype)

    if nk == 1:
        finish(part)
    else:
        k = pl.program_id(2)

        @pl.when(k == 0)
        def _():
            acc_ref[...] = part

        @pl.when(k > 0)
        def _():
            acc_ref[...] += part

        @pl.when(k == nk - 1)
        def _():
            finish(acc_ref[...])


def matmul(pairs, n, out_dtype, tm, tn, tk=None, residual=None, name="matmul"):
    m = pairs[0][0].shape[0]
    tm = min(tm, m)
    tn = min(tn, n)
    kdim = pairs[0][4]
    tk = kdim if tk is None else min(tk, kdim)
    nk = kdim // tk
    assert m % tm == 0 and n % tn == 0 and kdim % tk == 0
    in_specs, args = [], []
    for a, acb, b, brb, kd in pairs:
        assert kd == kdim
        in_specs.append(pl.BlockSpec((tm, tk), functools.partial(lambda i, j, k, o: (i, o + k), o=acb * nk)))
        in_specs.append(pl.BlockSpec((tk, tn), functools.partial(lambda i, j, k, o: (o + k, j), o=brb * nk)))
        args += [a, b]
    if residual is not None:
        in_specs.append(pl.BlockSpec((tm, tn), lambda i, j, k: (i, j)))
        args.append(residual)
    return pl.pallas_call(
        functools.partial(_mm_kernel, n_pairs=len(pairs), nk=nk, has_res=residual is not None),
        out_shape=jax.ShapeDtypeStruct((m, n), out_dtype),
        grid=(m // tm, n // tn, nk),
        in_specs=in_specs,
        out_specs=pl.BlockSpec((tm, tn), lambda i, j, k: (i, j)),
        scratch_shapes=[pltpu.VMEM((tm, tn), F32)] if nk > 1 else [],
        compiler_params=_cparams(("parallel", "parallel", "arbitrary")),
        name=name,
    )(*args)


def _swiglu_kernel(a_ref, bg_ref, bu_ref, o_ref):
    a = a_ref[...]
    gate = _dot(a, bg_ref[...])
    up = _dot(a, bu_ref[...])
    o_ref[...] = (gate * _sigmoid(gate) * up).astype(o_ref.dtype)


def matmul_swiglu(a, w, d_ff, tm, tn):
    m, k = a.shape
    tm = min(tm, m)
    tn = min(tn, d_ff)
    nj = d_ff // tn
    return pl.pallas_call(
        _swiglu_kernel,
        out_shape=jax.ShapeDtypeStruct((m, d_ff), BF16),
        grid=(m // tm, nj),
        in_specs=[pl.BlockSpec((tm, k), lambda i, j: (i, 0)),
                  pl.BlockSpec((k, tn), lambda i, j: (0, j)),
                  pl.BlockSpec((k, tn), lambda i, j: (0, j + nj))],
        out_specs=pl.BlockSpec((tm, tn), lambda i, j: (i, j)),
        compiler_params=_cparams(("parallel", "parallel")),
        name="ffn_in_swiglu",
    )(a, w, w)


def _dsa_kernel(q_ref, qi_ref, sm_ref, kidx_ref, ckv_ref, wuk_ref, wuv_ref, o_ref,
                key_ref, bias_ref, logit_ref, acc_ref, ql_ref, wb_ref, mx_ref, l_ref,
                *, n_heads, n_idx_heads, d_idx, dh, dv, tks, tk, group, topk):
    qb = q_ref.shape[0]
    i = pl.program_id(1)
    n_keys = (i + 1) * qb
    nkt = (n_keys + tk - 1) // tk
    idx_scale = d_idx ** -0.5 * n_idx_heads ** -0.5
    att_scale = dh ** -0.5

    for h in range(n_idx_heads):
        wb_ref[h] = jnp.broadcast_to(sm_ref[:, h:h + 1] * idx_scale, (qb, LANE))

    def key_limit(width):
        row = lax.broadcasted_iota(I32, (qb, width), 0)
        return (((i * qb + row) >> CHUNK_SHIFT) + 1) << CHUNK_SHIFT

    lim_s = key_limit(tks)
    col_s = lax.broadcasted_iota(I32, (qb, tks), 1)

    def score_tile(t, carry):
        ks = pl.multiple_of(t * tks, tks)
        kblk = kidx_ref[pl.ds(ks, tks), :].astype(BF16)
        parts = []
        for c in range(tks // LANE):
            parts.append(jnp.zeros((qb, LANE), F32))
        for h in range(n_idx_heads):
            r = _dot_nt(qi_ref[:, h * d_idx:(h + 1) * d_idx], kblk)
            wbh = wb_ref[h]
            for c in range(tks // LANE):
                parts[c] = parts[c] + jnp.maximum(r[:, c * LANE:(c + 1) * LANE], 0.0) * wbh
        sc = jnp.concatenate(parts, axis=1)
        bits = pltpu.bitcast(sc, I32)
        key = bits ^ ((bits >> 31) & 0x7FFFFFFF)
        key = jnp.where(ks + col_s < lim_s, key, INT_MIN)
        key_ref[:, pl.ds(ks, tks)] = key
        return carry

    lax.fori_loop(0, nkt * (tk // tks), score_tile, 0)

    lane_i = lax.broadcasted_iota(I32, (qb, LANE), 1)

    def count(pred):
        def body(t, c):
            ks = pl.multiple_of(t * tk, tk)
            kk = key_ref[:, pl.ds(ks, tk)]
            for cc in range(tk // LANE):
                c = c + jnp.where(pred(kk[:, cc * LANE:(cc + 1) * LANE], ks + cc * LANE + lane_i), 1.0, 0.0)
            return c
        c = lax.fori_loop(0, nkt, body, jnp.zeros((qb, LANE), F32))
        return jnp.sum(c, axis=-1, keepdims=True)

    def rep(x):
        return jnp.broadcast_to(x, (qb, LANE))

    kf = float(topk)
    t0 = rep(jnp.where(count(lambda k, p: k >= 0) >= kf, 0, INT_MIN)).astype(I32)

    def bisect(j, t):
        cand = t + (jnp.int32(1) << (30 - j))
        return jnp.where(count(lambda k, p: k >= cand) >= kf, cand, t)

    thr = lax.fori_loop(0, 31, bisect, t0)
    thr = jnp.maximum(thr, INT_MIN + 1)

    n_ge = count(lambda k, p: k >= thr)

    def tie_search():
        need = kf - count(lambda k, p: k > thr)

        def step(j, lo):
            cand = lo + (jnp.int32(1) << (pos_bits - 1 - j))
            below = count(lambda k, p: (k == thr) & (p < cand))
            return jnp.where(below < need, cand, lo)

        return lax.fori_loop(0, pos_bits, step, jnp.zeros((qb, LANE), I32))

    pos_bits = max(1, (key_ref.shape[1] - 1).bit_length())
    last = lax.cond(jnp.max(n_ge) > kf, tie_search, lambda: jnp.full((qb, LANE), key_ref.shape[1], I32))

    def bias_tile(t, carry):
        ks = pl.multiple_of(t * tk, tk)
        kk = key_ref[:, pl.ds(ks, tk)]
        for cc in range(tk // LANE):
            k = kk[:, cc * LANE:(cc + 1) * LANE]
            tied = jnp.where(ks + cc * LANE + lane_i <= last, 0.0, NEG_BIG)
            bias_ref[:, pl.ds(ks + cc * LANE, LANE)] = jnp.where(k > thr, 0.0, jnp.where(k == thr, tied, NEG_BIG))
        return carry

    lax.fori_loop(0, nkt, bias_tile, 0)

    rows = group * qb
    for g in range(n_heads // group):
        for hh in range(group):
            h = g * group + hh
            ql = _dot_nt(q_ref[:, h * dh:(h + 1) * dh], wuk_ref[h]) * att_scale
            ql_ref[hh * qb:(hh + 1) * qb, :] = ql.astype(BF16)
        mx_ref[...] = jnp.full((rows, LANE), NEG_BIG, F32)

        def pass_a(t, carry):
            ks = pl.multiple_of(t * tk, tk)
            s = _dot_nt(ql_ref[...], ckv_ref[pl.ds(ks, tk), :])
            b = bias_ref[:, pl.ds(ks, tk)]
            for hh in range(group):
                sh = s[hh * qb:(hh + 1) * qb, :] + b
                logit_ref[hh * qb:(hh + 1) * qb, pl.ds(ks, tk)] = sh
                m = mx_ref[hh * qb:(hh + 1) * qb, :]
                for cc in range(tk // LANE):
                    m = jnp.maximum(m, sh[:, cc * LANE:(cc + 1) * LANE])
                mx_ref[hh * qb:(hh + 1) * qb, :] = m
            return carry

        lax.fori_loop(0, nkt, pass_a, 0)
        mrow = jnp.max(mx_ref[...], axis=-1, keepdims=True)
        mx_ref[...] = jnp.broadcast_to(mrow, (rows, LANE))
        l_ref[...] = jnp.zeros((rows, LANE), F32)
        acc_ref[...] = jnp.zeros(acc_ref.shape, F32)

        def pass_b(t, carry):
            ks = pl.multiple_of(t * tk, tk)
            s = logit_ref[:, pl.ds(ks, tk)]
            m = mx_ref[...]
            ps = []
            lsum = l_ref[...]
            for cc in range(tk // LANE):
                p = jnp.exp(s[:, cc * LANE:(cc + 1) * LANE] - m)
                lsum = lsum + p
                ps.append(p.astype(BF16))
            l_ref[...] = lsum
            p_all = jnp.concatenate(ps, axis=1)
            acc_ref[...] += _dot(p_all, ckv_ref[pl.ds(ks, tk), :])
            return carry

        lax.fori_loop(0, nkt, pass_b, 0)
        lrow = jnp.sum(l_ref[...], axis=-1, keepdims=True)
        o_lat = (acc_ref[...] / lrow).astype(BF16)
        for hh in range(group):
            h = g * group + hh
            out = _dot(o_lat[hh * qb:(hh + 1) * qb, :], wuv_ref[h])
            o_ref[:, h * dv:(h + 1) * dv] = out.astype(o_ref.dtype)


def dsa_attention(qcat, proj_a, ckvn, w_uk, w_uv, batch, seq, *, n_heads, n_idx_heads, d_idx, dh, dv,
                  kidx_col_block, small_col_block):
    t_tok = batch * seq
    kv_lora = ckvn.shape[1]
    qb = min(Q_BLOCK, seq)
    nb = seq // qb
    tk = min(512, seq)
    tks = min(256, tk)
    group = 4
    topk = min(TOPK_MAX, seq // 4)
    qw = n_heads * dh
    qiw = n_idx_heads * d_idx
    assert qw == qiw
    kern = functools.partial(_dsa_kernel, n_heads=n_heads, n_idx_heads=n_idx_heads, d_idx=d_idx, dh=dh, dv=dv,
                             tks=tks, tk=tk, group=group, topk=topk)
    return pl.pallas_call(
        kern,
        out_shape=jax.ShapeDtypeStruct((t_tok, n_heads * dv), BF16),
        grid=(batch, nb),
        in_specs=[
            pl.BlockSpec((qb, qw), lambda b, i: (b * nb + i, 0)),
            pl.BlockSpec((qb, qiw), lambda b, i: (b * nb + i, 1)),
            pl.BlockSpec((qb, LANE), lambda b, i: (b * nb + i, small_col_block)),
            pl.BlockSpec((seq, d_idx), lambda b, i: (b, kidx_col_block)),
            pl.BlockSpec((seq, kv_lora), lambda b, i: (b, 0)),
            pl.BlockSpec(w_uk.shape, lambda b, i: (0, 0, 0)),
            pl.BlockSpec(w_uv.shape, lambda b, i: (0, 0, 0)),
        ],
        out_specs=pl.BlockSpec((qb, n_heads * dv), lambda b, i: (b * nb + i, 0)),
        scratch_shapes=[
            pltpu.VMEM((qb, seq), I32),
            pltpu.VMEM((qb, seq), F32),
            pltpu.VMEM((group * qb, seq), F32),
            pltpu.VMEM((group * qb, kv_lora), F32),
            pltpu.VMEM((group * qb, kv_lora), BF16),
            pltpu.VMEM((n_idx_heads, qb, LANE), F32),
            pltpu.VMEM((group * qb, LANE), F32),
            pltpu.VMEM((group * qb, LANE), F32),
        ],
        compiler_params=_cparams(("arbitrary", "arbitrary")),
        name="dsa_attention",
    )(qcat, qcat, proj_a, proj_a, ckvn, w_uk, w_uv)


def _split_bf16(x):
    hi = x.astype(BF16)
    lo = (x - hi.astype(F32)).astype(BF16)
    return hi, lo


def _dot3(a, b):
    ah, al = _split_bf16(a)
    bh, bl = _split_bf16(b)
    lhs = jnp.concatenate([ah, ah, al], axis=1)
    rhs = jnp.concatenate([bh, bl, bh], axis=0)
    return _dot(lhs, rhs)


def _cumsum_rows(x):
    n = x.shape[0]
    row = lax.broadcasted_iota(I32, x.shape, 0)
    s = 1
    while s < n:
        x = x + jnp.where(row >= s, pltpu.roll(x, s, 0), 0.0)
        s *= 2
    return x


def _gdn_kernel(xq_ref, xk_ref, xv_ref, z_ref, gate_ref, cwq_ref, cwk_ref, cwv_ref, alog_ref, dtb_ref, gn_ref,
                o_ref, ext_ref, state_ref, *, pairs, dk, b_col, a_col, conv_width):
    c = CHUNK
    assert dk == 2 * c
    n = pl.program_id(2)
    hg = pl.program_id(1)
    width = pairs * 2 * dk
    halo = 8

    @pl.when(n == 0)
    def _():
        ext_ref[:, 0:halo, :] = jnp.zeros((3, halo, width), F32)
        state_ref[...] = jnp.zeros(state_ref.shape, F32)

    def conv_silu(idx, x_ref, cw_ref):
        ext_ref[idx, halo:halo + c, :] = x_ref[...]
        y = jnp.zeros((c, width), F32)
        for j in range(conv_width):
            off = halo - (conv_width - 1) + j
            y = y + ext_ref[idx, off:off + c, :] * cw_ref[j:j + 1, :]
        ext_ref[idx, 0:halo, :] = ext_ref[idx, c:c + halo, :]
        return y * _sigmoid(y)

    xq = conv_silu(0, xq_ref, cwq_ref)
    xk = conv_silu(1, xk_ref, cwk_ref)
    xv = conv_silu(2, xv_ref, cwv_ref)

    gates = gate_ref[...]
    beta_all = _sigmoid(gates)
    sp = jnp.maximum(gates + dtb_ref[...], 0.0) + jnp.log(1.0 + jnp.exp(-jnp.abs(gates + dtb_ref[...])))
    g_all = -jnp.exp(alog_ref[...]) * sp
    gcum_all = _cumsum_rows(g_all)

    r2 = lax.broadcasted_iota(I32, (2 * c, 2 * c), 0)
    c2 = lax.broadcasted_iota(I32, (2 * c, 2 * c), 1)
    same = (r2 ^ c2) < c
    incl = same & (r2 >= c2)
    strict = same & (r2 > c2)
    top = lax.broadcasted_iota(I32, (2 * c, dk), 0) < c

    def stack(x, p):
        return jnp.concatenate([x[:, (2 * p) * dk:(2 * p + 1) * dk], x[:, (2 * p + 1) * dk:(2 * p + 2) * dk]], axis=0)

    def col(x, lane):
        return jnp.broadcast_to(x[:, lane:lane + 1], (c, dk))

    for p in range(pairs):
        h0 = (hg * pairs + p) * 2
        q = stack(xq, p)
        k = stack(xk, p)
        v = stack(xv, p)
        q = q * lax.rsqrt(jnp.sum(q * q, axis=-1, keepdims=True) + EPS) * (dk ** -0.5)
        k = k * lax.rsqrt(jnp.sum(k * k, axis=-1, keepdims=True) + EPS)

        lane = lax.broadcasted_iota(I32, (c, LANE), 1)

        def pick(x, base):
            a = jnp.sum(jnp.where(lane == base + h0, x, 0.0), axis=-1, keepdims=True)
            b = jnp.sum(jnp.where(lane == base + h0 + 1, x, 0.0), axis=-1, keepdims=True)
            return jnp.concatenate([jnp.broadcast_to(a, (c, dk)), jnp.broadcast_to(b, (c, dk))], axis=0)

        beta = pick(beta_all, b_col)
        gc = pick(gcum_all, a_col)
        glast = jnp.concatenate([jnp.broadcast_to(gc[c - 1:c, :], (c, dk)),
                                 jnp.broadcast_to(gc[2 * c - 1:2 * c, :], (c, dk))], axis=0)
        eg = jnp.exp(gc)
        decay = jnp.exp(jnp.where(incl, gc - gc.T, -jnp.inf))

        kb = k * beta
        kk = _dot_nt(kb.astype(BF16), k.astype(BF16))
        nmat = jnp.where(strict, -(kk * decay), 0.0)
        x = jnp.concatenate([v * beta, kb * eg], axis=1)
        pw = nmat
        for lvl in range(6):
            x = x + _dot3(pw, x)
            if lvl < 5:
                pw = _dot3(pw, pw)
        u = x[:, 0:dk]
        w = x[:, dk:2 * dk]

        qk = _dot_nt(q.astype(BF16), k.astype(BF16))
        amat = jnp.where(incl, qk * decay, 0.0)
        q_dec = q * eg
        k_dec = k * jnp.exp(glast - gc)
        cd = jnp.exp(glast)

        s2 = state_ref[p]
        s2b = s2.astype(BF16)
        ws = _dot(w.astype(BF16), s2b)
        ws = jnp.where(top, ws[:, 0:dk], ws[:, dk:2 * dk])
        v_new = u - ws
        qs = _dot(q_dec.astype(BF16), s2b)
        qs = jnp.where(top, qs[:, 0:dk], qs[:, dk:2 * dk])
        vb = v_new.astype(BF16)
        o = qs + _dot(amat.astype(BF16), vb)
        zero = jnp.zeros_like(vb)
        vexp = jnp.concatenate([jnp.where(top, vb, zero), jnp.where(top, zero, vb)], axis=1)
        upd = _dot(k_dec.T.astype(BF16), vexp)
        cd_a = cd[0:1, :]
        cd_b = cd[c:c + 1, :]
        cdrow = jnp.concatenate([cd_a, cd_b], axis=1)
        state_ref[p] = s2 * cdrow + upd

        on = o * lax.rsqrt(jnp.mean(o * o, axis=-1, keepdims=True) + EPS) * gn_ref[...]
        zz = stack(z_ref[...], p)
        y = on * (zz * _sigmoid(zz))
        o_ref[:, (2 * p) * dk:(2 * p + 1) * dk] = y[0:c, :].astype(o_ref.dtype)
        o_ref[:, (2 * p + 1) * dk:(2 * p + 2) * dk] = y[c:2 * c, :].astype(o_ref.dtype)


def gated_deltanet(proj_b, proj_small, conv_w, a_log, dt_bias, g_out, batch, seq, *, n_heads, dk,
                   small_col_block, small_row_array_cols, b_col, a_col, pairs):
    t_tok = batch * seq
    nchunk = seq // CHUNK
    width = pairs * 2 * dk
    ng = n_heads // (2 * pairs)
    hw = n_heads * dk
    sec = hw // width
    conv_width = conv_w.shape[0]
    alog_row = jnp.zeros((1, LANE), F32).at[0, a_col:a_col + n_heads].set(a_log.astype(F32))
    dtb_row = jnp.zeros((1, LANE), F32).at[0, a_col:a_col + n_heads].set(dt_bias.astype(F32))
    kern = functools.partial(_gdn_kernel, pairs=pairs, dk=dk, b_col=b_col, a_col=a_col, conv_width=conv_width)
    row = lambda b, g, n: b * nchunk + n
    return pl.pallas_call(
        kern,
        out_shape=jax.ShapeDtypeStruct((t_tok, hw), BF16),
        grid=(batch, ng, nchunk),
        in_specs=[
            pl.BlockSpec((CHUNK, width), lambda b, g, n: (row(b, g, n), g)),
            pl.BlockSpec((CHUNK, width), lambda b, g, n: (row(b, g, n), sec + g)),
            pl.BlockSpec((CHUNK, width), lambda b, g, n: (row(b, g, n), 2 * sec + g)),
            pl.BlockSpec((CHUNK, width), lambda b, g, n: (row(b, g, n), 3 * sec + g)),
            pl.BlockSpec((CHUNK, LANE), lambda b, g, n: (row(b, g, n), small_col_block)),
            pl.BlockSpec((conv_width, width), lambda b, g, n: (0, g)),
            pl.BlockSpec((conv_width, width), lambda b, g, n: (0, sec + g)),
            pl.BlockSpec((conv_width, width), lambda b, g, n: (0, 2 * sec + g)),
            pl.BlockSpec((1, LANE), lambda b, g, n: (0, 0)),
            pl.BlockSpec((1, LANE), lambda b, g, n: (0, 0)),
            pl.BlockSpec((1, dk), lambda b, g, n: (0, 0)),
        ],
        out_specs=pl.BlockSpec((CHUNK, width), lambda b, g, n: (row(b, g, n), g)),
        scratch_shapes=[pltpu.VMEM((3, CHUNK + 8, width), F32),
                        pltpu.VMEM((pairs, dk, 2 * dk), F32)],
        compiler_params=_cparams(("arbitrary", "arbitrary", "arbitrary")),
        name="gated_deltanet",
    )(proj_b, proj_b, proj_b, proj_b, proj_small, conv_w, conv_w, conv_w, alog_row, dtb_row,
      g_out.reshape(1, dk).astype(F32))


def _cross_kernel(q_ref, k_ref, v_ref, wco_ref, h_ref, g_ref, h_out_ref, n_out_ref, *, n_heads, dh):
    scale = dh ** -0.5
    outs = []
    for h in range(n_heads):
        s = _dot_nt(q_ref[:, h * dh:(h + 1) * dh], k_ref[:, h * dh:(h + 1) * dh]) * scale
        m = jnp.max(s, axis=-1, keepdims=True)
        p = jnp.exp(s - m)
        l = jnp.sum(p, axis=-1, keepdims=True)
        p = (p / l).astype(BF16)
        outs.append(_dot(p, v_ref[:, h * dh:(h + 1) * dh]).astype(BF16))
    o = jnp.concatenate(outs, axis=1)
    hn = h_ref[...] + _dot(o, wco_ref[...])
    h_out_ref[...] = hn
    ms = jnp.mean(hn * hn, axis=-1, keepdims=True)
    n_out_ref[...] = (hn * lax.rsqrt(ms + EPS) * g_ref[...]).astype(n_out_ref.dtype)


def cross_attention(qx, kvx, w_co, h, g_next, batch, seq, n_mem, *, n_heads, dh, tq=256):
    t_tok, d = h.shape
    tq = min(tq, seq)
    nq = seq // tq
    hw = n_heads * dh
    kern = functools.partial(_cross_kernel, n_heads=n_heads, dh=dh)
    return pl.pallas_call(
        kern,
        out_shape=(jax.ShapeDtypeStruct((t_tok, d), F32), jax.ShapeDtypeStruct((t_tok, d), BF16)),
        grid=(batch, nq),
        in_specs=[
            pl.BlockSpec((tq, hw), lambda b, i: (b * nq + i, 0)),
            pl.BlockSpec((n_mem, hw), lambda b, i: (b, 0)),
            pl.BlockSpec((n_mem, hw), lambda b, i: (b, 1)),
            pl.BlockSpec((hw, d), lambda b, i: (0, 0)),
            pl.BlockSpec((tq, d), lambda b, i: (b * nq + i, 0)),
            pl.BlockSpec((1, d), lambda b, i: (0, 0)),
        ],
        out_specs=(pl.BlockSpec((tq, d), lambda b, i: (b * nq + i, 0)),
                   pl.BlockSpec((tq, d), lambda b, i: (b * nq + i, 0))),
        compiler_params=_cparams(("parallel", "parallel")),
        name="cross_attention",
    )(qx, kvx, kvx, w_co, h, g_next.reshape(1, d).astype(F32))


def _pad_cols(w, n):
    return jnp.pad(w, ((0, 0), (0, n - w.shape[1])))


def kernel(x, mem, attn_norm_g, w_in, qa_norm_g, w_qb, kv_norm_g, w_uk, w_uv, w_iq, conv_w, a_log, dt_bias,
           delta_norm_g, w_o, cross_norm_g, mem_norm_g, w_cq, w_ckv, w_co, ffn_norm_g, w_ffn_in, w_ffn_out,
           final_norm_g):
    batch, seq, d = x.shape
    n_mem = mem.shape[1]
    depth = w_in.shape[0]
    q_lora = qa_norm_g.shape[1]
    kv_lora = kv_norm_g.shape[1]
    h_a, _, dh_a = w_uk.shape[1:]
    dv_a = w_uv.shape[3]
    h_idx = w_iq.shape[2] // 128
    d_idx = w_iq.shape[2] // h_idx
    h_b = a_log.shape[1]
    dk_b = delta_norm_g.shape[1]
    h_x = w_cq.shape[2] // 128
    dh_x = w_cq.shape[2] // h_x
    d_ff = w_ffn_out.shape[1]
    t_tok = batch * seq
    hw_b = h_b * dk_b

    h = x.reshape(t_tok, d)
    memf = mem.reshape(batch * n_mem, d)
    for l in range(depth):
        win = w_in[l]
        o_qa, o_ckv, o_kidx, o_widx = 0, q_lora, q_lora + kv_lora, q_lora + kv_lora + d_idx
        o_qkv = o_widx + h_idx
        o_z = o_qkv + 3 * hw_b
        o_b = o_z + hw_b
        o_a = o_b + h_b
        small = jnp.concatenate([win[:, o_widx:o_widx + h_idx], win[:, o_b:o_b + h_b], win[:, o_a:o_a + h_b]], axis=1)
        w_a = jnp.concatenate([win[:, o_qa:o_widx], _pad_cols(small, LANE)], axis=1).astype(BF16)
        w_b = win[:, o_qkv:o_qkv + 4 * hw_b].astype(BF16)
        na = w_a.shape[1]
        assert q_lora % kv_lora == 0 and (q_lora + kv_lora) % d_idx == 0 and d_idx == LANE
        kidx_cb = (q_lora + kv_lora) // d_idx
        small_cb = kidx_cb + 1
        w_q2 = jnp.concatenate([w_qb[l], w_iq[l]], axis=1).astype(BF16)
        w_uk_b = w_uk[l].astype(BF16)
        w_uv_b = w_uv[l].astype(BF16)
        w_o_b = w_o[l].astype(BF16)
        w_cq_b = w_cq[l].astype(BF16)
        w_ckv_b = w_ckv[l].astype(BF16)
        w_co_b = w_co[l].astype(BF16)
        ffp = -(-d_ff // 512) * 512
        wfi = w_ffn_in[l]
        w_fi_b = jnp.concatenate([_pad_cols(wfi[:, :d_ff], ffp), _pad_cols(wfi[:, d_ff:], ffp)], axis=1).astype(BF16)
        w_fo_b = jnp.pad(w_ffn_out[l], ((0, ffp - d_ff), (0, 0))).astype(BF16)

        n0 = rmsnorm_cols(h, attn_norm_g[l], d, 0, BF16)
        tn_a = na // 3 if na % (3 * LANE) == 0 else na
        proj_a = matmul([(n0, 0, w_a, 0, d)], na, F32, tm=1024, tn=tn_a, name="in_proj_a")
        proj_b = matmul([(n0, 0, w_b, 0, d)], 4 * hw_b, F32, tm=1024, tn=512, name="in_proj_b")
        qan = rmsnorm_cols(proj_a, qa_norm_g[l], q_lora, 0, BF16)
        ckvn = rmsnorm_cols(proj_a, kv_norm_g[l], kv_lora, q_lora // kv_lora, BF16)
        qcat = matmul([(qan, 0, w_q2, 0, q_lora)], w_q2.shape[1], BF16, tm=1024, tn=1024, name="q_proj")
        y_a = dsa_attention(qcat, proj_a, ckvn, w_uk_b, w_uv_b, batch, seq, n_heads=h_a, n_idx_heads=h_idx,
                            d_idx=d_idx, dh=dh_a, dv=dv_a, kidx_col_block=kidx_cb, small_col_block=small_cb)
        y_b = gated_deltanet(proj_b, proj_a, conv_w[l], a_log[l], dt_bias[l], delta_norm_g[l], batch, seq,
                             n_heads=h_b, dk=dk_b, small_col_block=small_cb, small_row_array_cols=na,
                             b_col=h_idx, a_col=h_idx + h_b, pairs=2)
        k_a = h_a * dv_a
        assert k_a == hw_b
        h = matmul([(y_a, 0, w_o_b, 0, k_a), (y_b, 0, w_o_b, 1, k_a)], d, F32, tm=1024, tn=512, residual=h,
                   name="out_proj")

        memn = rmsnorm_cols(memf, mem_norm_g[l], d, 0, BF16)
        kvx = matmul([(memn, 0, w_ckv_b, 0, d)], w_ckv_b.shape[1], BF16, tm=1024, tn=512, name="mem_kv_proj")
        hn = rmsnorm_cols(h, cross_norm_g[l], d, 0, BF16)
        qx = matmul([(hn, 0, w_cq_b, 0, d)], w_cq_b.shape[1], BF16, tm=1024, tn=512, name="cross_q_proj")
        h, n2 = cross_attention(qx, kvx, w_co_b, h, ffn_norm_g[l], batch, seq, n_mem, n_heads=h_x, dh=dh_x)

        act = matmul_swiglu(n2, w_fi_b, ffp, tm=1024, tn=512)
        tk_f = ffp // 4 if ffp % (4 * LANE) == 0 else ffp
        h = matmul([(act, 0, w_fo_b, 0, ffp)], d, F32, tm=1024, tn=512, tk=tk_f, residual=h, name="ffn_out")
    out = rmsnorm_cols(h, final_norm_g, d, 0, F32)
    return out.reshape(batch, seq, d)
```

```python
import functools

import jax
import jax.numpy as jnp
from jax import lax
from jax.experimental import pallas as pl
from jax.experimental.pallas import tpu as pltpu

F32 = jnp.float32
BF16 = jnp.bfloat16
I32 = jnp.int32

EPS = 1e-6
CHUNK = 64
Q_BLOCK = 128
TOPK_MAX = 256
LANE = 128
VMEM_LIMIT = 56 * 1024 * 1024
INT_MIN = -2 ** 31
CHUNK_SHIFT = CHUNK.bit_length() - 1
assert 1 << CHUNK_SHIFT == CHUNK
NEG_BIG = -1e30


def _cparams(sem):
    return pltpu.CompilerParams(dimension_semantics=sem, vmem_limit_bytes=VMEM_LIMIT)


def _dot(a, b):
    return jnp.dot(a, b, preferred_element_type=F32)


def _dot_nt(a, b):
    return lax.dot_general(a, b, (((1,), (1,)), ((), ())), preferred_element_type=F32)


def _sigmoid(x):
    return 1.0 / (1.0 + jnp.exp(-x))


def _rmsnorm_kernel(x_ref, g_ref, o_ref):
    x = x_ref[...].astype(F32)
    ms = jnp.mean(x * x, axis=-1, keepdims=True)
    o_ref[...] = (x * lax.rsqrt(ms + EPS) * g_ref[...]).astype(o_ref.dtype)


def rmsnorm_cols(x, g, width, col_block, out_dtype, tr=256):
    m = x.shape[0]
    tr = min(tr, m)
    return pl.pallas_call(
        _rmsnorm_kernel,
        out_shape=jax.ShapeDtypeStruct((m, width), out_dtype),
        grid=(m // tr,),
        in_specs=[pl.BlockSpec((tr, width), lambda i: (i, col_block)),
                  pl.BlockSpec((1, width), lambda i: (0, 0))],
        out_specs=pl.BlockSpec((tr, width), lambda i: (i, 0)),
        compiler_params=_cparams(("parallel",)),
        name="rmsnorm",
    )(x, g.reshape(1, width).astype(F32))


def _mm_kernel(*refs, n_pairs, nk, has_res):
    a_refs = refs[0:2 * n_pairs:2]
    b_refs = refs[1:2 * n_pairs:2]
    pos = 2 * n_pairs
    r_ref = refs[pos] if has_res else None
    pos += int(has_res)
    o_ref = refs[pos]
    acc_ref = refs[pos + 1] if nk > 1 else None

    part = _dot(a_refs[0][...], b_refs[0][...])
    for a_ref, b_ref in zip(a_refs[1:], b_refs[1:]):
        part = part + _dot(a_ref[...], b_ref[...])

    def finish(acc):
        if has_res:
            acc = acc + r_ref[...]
        o_ref[...] = acc.astype(o_ref.dtype)

    if nk == 1:
        finish(part)
    else:
        k = pl.program_id(2)

        @pl.when(k == 0)
        def _():
            acc_ref[...] = part

        @pl.when(k > 0)
        def _():
            acc_ref[...] += part

        @pl.when(k == nk - 1)
        def _():
            finish(acc_ref[...])


def matmul(pairs, n, out_dtype, tm, tn, tk=None, residual=None, name="matmul"):
    m = pairs[0][0].shape[0]
    tm = min(tm, m)
    tn = min(tn, n)
    kdim = pairs[0][4]
    tk = kdim if tk is None else min(tk, kdim)
    nk = kdim // tk
    assert m % tm == 0 and n % tn == 0 and kdim % tk == 0
    in_specs, args = [], []
    for a, acb, b, brb, kd in pairs:
        assert kd == kdim
        in_specs.append(pl.BlockSpec((tm, tk), functools.partial(lambda i, j, k, o: (i, o + k), o=acb * nk)))
        in_specs.append(pl.BlockSpec((tk, tn), functools.partial(lambda i, j, k, o: (o + k, j), o=brb * nk)))
        args += [a, b]
    if residual is not None:
        in_specs.append(pl.BlockSpec((tm, tn), lambda i, j, k: (i, j)))
        args.append(residual)
    return pl.pallas_call(
        functools.partial(_mm_kernel, n_pairs=len(pairs), nk=nk, has_res=residual is not None),
        out_shape=jax.ShapeDtypeStruct((m, n), out_dtype),
        grid=(m // tm, n // tn, nk),
        in_specs=in_specs,
        out_specs=pl.BlockSpec((tm, tn), lambda i, j, k: (i, j)),
        scratch_shapes=[pltpu.VMEM((tm, tn), F32)] if nk > 1 else [],
        compiler_params=_cparams(("parallel", "parallel", "arbitrary")),
        name=name,
    )(*args)


def _swiglu_kernel(a_ref, bg_ref, bu_ref, o_ref):
    a = a_ref[...]
    gate = _dot(a, bg_ref[...])
    up = _dot(a, bu_ref[...])
    o_ref[...] = (gate * _sigmoid(gate) * up).astype(o_ref.dtype)


def matmul_swiglu(a, w, d_ff, tm, tn):
    m, k = a.shape
    tm = min(tm, m)
    tn = min(tn, d_ff)
    nj = d_ff // tn
    return pl.pallas_call(
        _swiglu_kernel,
        out_shape=jax.ShapeDtypeStruct((m, d_ff), BF16),
        grid=(m // tm, nj),
        in_specs=[pl.BlockSpec((tm, k), lambda i, j: (i, 0)),
                  pl.BlockSpec((k, tn), lambda i, j: (0, j)),
                  pl.BlockSpec((k, tn), lambda i, j: (0, j + nj))],
        out_specs=pl.BlockSpec((tm, tn), lambda i, j: (i, j)),
        compiler_params=_cparams(("parallel", "parallel")),
        name="ffn_in_swiglu",
    )(a, w, w)


def _dsa_kernel(q_ref, qi_ref, sm_ref, kidx_ref, ckv_ref, wuk_ref, wuv_ref, o_ref,
                key_ref, bias_ref, s_ref, p_ref, acc_ref, ql_ref, qis_ref, wb_ref, m_ref, l_ref, al_ref,
                *, n_heads, n_idx_heads, d_idx, dh, dv, tk, topk):
    qb = q_ref.shape[0]
    i = pl.program_id(1)
    n_keys = (i + 1) * qb
    nkt = (n_keys + tk - 1) // tk
    idx_scale = d_idx ** -0.5 * n_idx_heads ** -0.5
    att_scale = dh ** -0.5
    nlc = tk // LANE

    for h in range(n_idx_heads):
        wb_ref[h] = jnp.broadcast_to(sm_ref[:, h:h + 1] * idx_scale, (qb, LANE))
        qis_ref[h * qb:(h + 1) * qb, :] = qi_ref[:, h * d_idx:(h + 1) * d_idx]

    row_q = lax.broadcasted_iota(I32, (qb, LANE), 0)
    lane_i = lax.broadcasted_iota(I32, (qb, LANE), 1)
    lim = (((i * qb + row_q) >> CHUNK_SHIFT) + 1) << CHUNK_SHIFT

    def score_tile(t, carry):
        ks = pl.multiple_of(t * tk, tk)
        kblk = kidx_ref[pl.ds(ks, tk), :].astype(BF16)
        s_ref[0:n_idx_heads * qb, :] = _dot_nt(qis_ref[...], kblk)
        for c in range(nlc):
            sc = jnp.zeros((qb, LANE), F32)
            for h in range(n_idx_heads):
                sc = sc + jnp.maximum(s_ref[h * qb:(h + 1) * qb, c * LANE:(c + 1) * LANE], 0.0) * wb_ref[h]
            bits = pltpu.bitcast(sc, I32)
            key = bits ^ ((bits >> 31) & 0x7FFFFFFF)
            key = jnp.where(ks + c * LANE + lane_i < lim, key, INT_MIN)
            key_ref[:, pl.ds(ks + c * LANE, LANE)] = key
        return carry

    lax.fori_loop(0, nkt, score_tile, 0)

    def count(pred):
        def body(t, c):
            ks = pl.multiple_of(t * tk, tk)
            kk = key_ref[:, pl.ds(ks, tk)]
            for cc in range(tk // LANE):
                c = c + jnp.where(pred(kk[:, cc * LANE:(cc + 1) * LANE], ks + cc * LANE + lane_i), 1.0, 0.0)
            return c
        c = lax.fori_loop(0, nkt, body, jnp.zeros((qb, LANE), F32))
        return jnp.sum(c, axis=-1, keepdims=True)

    def rep(x):
        return jnp.broadcast_to(x, (qb, LANE))

    kf = float(topk)
    t0 = rep(jnp.where(count(lambda k, p: k >= 0) >= kf, 0, INT_MIN)).astype(I32)

    def bisect(j, t):
        cand = t + (jnp.int32(1) << (30 - j))
        return jnp.where(count(lambda k, p: k >= cand) >= kf, cand, t)

    thr = lax.fori_loop(0, 31, bisect, t0)
    thr = jnp.maximum(thr, INT_MIN + 1)

    n_ge = count(lambda k, p: k >= thr)

    def tie_search():
        need = kf - count(lambda k, p: k > thr)

        def step(j, lo):
            cand = lo + (jnp.int32(1) << (pos_bits - 1 - j))
            below = count(lambda k, p: (k == thr) & (p < cand))
            return jnp.where(below < need, cand, lo)

        return lax.fori_loop(0, pos_bits, step, jnp.zeros((qb, LANE), I32))

    pos_bits = max(1, (key_ref.shape[1] - 1).bit_length())
    last = lax.cond(jnp.max(n_ge) > kf, tie_search, lambda: jnp.full((qb, LANE), key_ref.shape[1], I32))

    def bias_tile(t, carry):
        ks = pl.multiple_of(t * tk, tk)
        kk = key_ref[:, pl.ds(ks, tk)]
        for cc in range(tk // LANE):
            k = kk[:, cc * LANE:(cc + 1) * LANE]
            tied = jnp.where(ks + cc * LANE + lane_i <= last, 0.0, NEG_BIG)
            bias_ref[:, pl.ds(ks + cc * LANE, LANE)] = jnp.where(k > thr, 0.0, jnp.where(k == thr, tied, NEG_BIG))
        return carry

    lax.fori_loop(0, nkt, bias_tile, 0)

    rows = n_heads * qb
    for h in range(n_heads):
        ql = _dot_nt(q_ref[:, h * dh:(h + 1) * dh], wuk_ref[h]) * att_scale
        ql_ref[h * qb:(h + 1) * qb, :] = ql.astype(BF16)
    m_ref[...] = jnp.full((rows, LANE), NEG_BIG, F32)
    l_ref[...] = jnp.zeros((rows, LANE), F32)
    acc_ref[...] = jnp.zeros(acc_ref.shape, F32)

    def attend(t, carry):
        ks = pl.multiple_of(t * tk, tk)
        ckv_t = ckv_ref[pl.ds(ks, tk), :]
        s_ref[0:rows, :] = _dot_nt(ql_ref[...], ckv_t)
        for h in range(n_heads):
            r0 = h * qb
            s = [s_ref[r0:r0 + qb, c * LANE:(c + 1) * LANE] + bias_ref[:, pl.ds(ks + c * LANE, LANE)]
                 for c in range(nlc)]
            tmax = s[0]
            for c in range(1, nlc):
                tmax = jnp.maximum(tmax, s[c])
            m_old = m_ref[r0:r0 + qb, :]
            m_new = jnp.maximum(m_old, jnp.max(tmax, axis=-1, keepdims=True))
            alpha = jnp.exp(m_old - m_new)
            lsum = l_ref[r0:r0 + qb, :] * alpha
            for c in range(nlc):
                p = jnp.exp(s[c] - m_new)
                lsum = lsum + p
                p_ref[r0:r0 + qb, c * LANE:(c + 1) * LANE] = p.astype(BF16)
            m_ref[r0:r0 + qb, :] = m_new
            l_ref[r0:r0 + qb, :] = lsum
            al_ref[r0:r0 + qb, :] = alpha
        s_ref[0:rows, 0:ckv_t.shape[1]] = _dot(p_ref[...], ckv_t)
        for h in range(n_heads):
            r0 = h * qb
            alpha = al_ref[r0:r0 + qb, :]
            for c in range(ckv_t.shape[1] // LANE):
                cs = slice(c * LANE, (c + 1) * LANE)
                acc_ref[r0:r0 + qb, cs] = acc_ref[r0:r0 + qb, cs] * alpha + s_ref[r0:r0 + qb, cs]
        return carry

    lax.fori_loop(0, nkt, attend, 0)
    for h in range(n_heads):
        r0 = h * qb
        lrow = jnp.sum(l_ref[r0:r0 + qb, :], axis=-1, keepdims=True)
        o_lat = (acc_ref[r0:r0 + qb, :] / lrow).astype(BF16)
        o_ref[:, h * dv:(h + 1) * dv] = _dot(o_lat, wuv_ref[h]).astype(o_ref.dtype)


def dsa_attention(qcat, proj_a, ckvn, w_uk, w_uv, batch, seq, *, n_heads, n_idx_heads, d_idx, dh, dv,
                  kidx_col_block, small_col_block):
    t_tok = batch * seq
    kv_lora = ckvn.shape[1]
    qb = min(Q_BLOCK, seq)
    nb = seq // qb
    tk = min(512, seq)
    topk = min(TOPK_MAX, seq // 4)
    qw = n_heads * dh
    qiw = n_idx_heads * d_idx
    assert qw == qiw and kv_lora <= tk
    srows = max(n_heads, n_idx_heads) * qb
    kern = functools.partial(_dsa_kernel, n_heads=n_heads, n_idx_heads=n_idx_heads, d_idx=d_idx, dh=dh, dv=dv,
                             tk=tk, topk=topk)
    once = pl.Buffered(1)
    return pl.pallas_call(
        kern,
        out_shape=jax.ShapeDtypeStruct((t_tok, n_heads * dv), BF16),
        grid=(batch, nb),
        in_specs=[
            pl.BlockSpec((qb, qw), lambda b, i: (b * nb + i, 0)),
            pl.BlockSpec((qb, qiw), lambda b, i: (b * nb + i, 1)),
            pl.BlockSpec((qb, LANE), lambda b, i: (b * nb + i, small_col_block)),
            pl.BlockSpec((seq, d_idx), lambda b, i: (b, kidx_col_block), pipeline_mode=once),
            pl.BlockSpec((seq, kv_lora), lambda b, i: (b, 0), pipeline_mode=once),
            pl.BlockSpec(w_uk.shape, lambda b, i: (0, 0, 0), pipeline_mode=once),
            pl.BlockSpec(w_uv.shape, lambda b, i: (0, 0, 0), pipeline_mode=once),
        ],
        out_specs=pl.BlockSpec((qb, n_heads * dv), lambda b, i: (b * nb + i, 0)),
        scratch_shapes=[
            pltpu.VMEM((qb, seq), I32),
            pltpu.VMEM((qb, seq), F32),
            pltpu.VMEM((srows, tk), F32),
            pltpu.VMEM((n_heads * qb, tk), BF16),
            pltpu.VMEM((n_heads * qb, kv_lora), F32),
            pltpu.VMEM((n_heads * qb, kv_lora), BF16),
            pltpu.VMEM((n_idx_heads * qb, d_idx), BF16),
            pltpu.VMEM((n_idx_heads, qb, LANE), F32),
            pltpu.VMEM((n_heads * qb, LANE), F32),
            pltpu.VMEM((n_heads * qb, LANE), F32),
            pltpu.VMEM((n_heads * qb, LANE), F32),
        ],
        compiler_params=_cparams(("arbitrary", "arbitrary")),
        name="dsa_attention",
    )(qcat, qcat, proj_a, proj_a, ckvn, w_uk, w_uv)


def _split_bf16(x):
    hi = x.astype(BF16)
    lo = (x - hi.astype(F32)).astype(BF16)
    return hi, lo


def _dot3(a, b):
    ah, al = _split_bf16(a)
    bh, bl = _split_bf16(b)
    lhs = jnp.concatenate([ah, ah, al], axis=1)
    rhs = jnp.concatenate([bh, bl, bh], axis=0)
    return _dot(lhs, rhs)


def _cumsum_rows(x):
    n = x.shape[0]
    row = lax.broadcasted_iota(I32, x.shape, 0)
    s = 1
    while s < n:
        x = x + jnp.where(row >= s, pltpu.roll(x, s, 0), 0.0)
        s *= 2
    return x


def _gdn_kernel(xq_ref, xk_ref, xv_ref, z_ref, gate_ref, cwq_ref, cwk_ref, cwv_ref, alog_ref, dtb_ref, gn_ref,
                o_ref, ext_ref, state_ref, *, pairs, dk, b_col, a_col, conv_width):
    c = CHUNK
    assert dk == 2 * c
    n = pl.program_id(2)
    hg = pl.program_id(1)
    width = pairs * 2 * dk
    halo = 8

    @pl.when(n == 0)
    def _():
        ext_ref[:, 0:halo, :] = jnp.zeros((3, halo, width), F32)
        state_ref[...] = jnp.zeros(state_ref.shape, F32)

    prs = range(pairs)

    def conv_silu(idx, cw_ref, p):
        halves = []
        for hh in range(2):
            lo = (2 * p + hh) * dk
            y = jnp.zeros((c, dk), F32)
            for j in range(conv_width):
                off = halo - (conv_width - 1) + j
                y = y + ext_ref[idx, off:off + c, lo:lo + dk] * cw_ref[j:j + 1, lo:lo + dk]
            halves.append(y * _sigmoid(y))
        return jnp.concatenate(halves, axis=0)

    for idx, x_ref in enumerate((xq_ref, xk_ref, xv_ref)):
        ext_ref[idx, halo:halo + c, :] = x_ref[...]
    q = [conv_silu(0, cwq_ref, p) for p in prs]
    k = [conv_silu(1, cwk_ref, p) for p in prs]
    v = [conv_silu(2, cwv_ref, p) for p in prs]
    for idx in range(3):
        ext_ref[idx, 0:halo, :] = ext_ref[idx, c:c + halo, :]

    gates = gate_ref[...]
    beta_all = _sigmoid(gates)
    sp = jnp.maximum(gates + dtb_ref[...], 0.0) + jnp.log(1.0 + jnp.exp(-jnp.abs(gates + dtb_ref[...])))
    g_all = -jnp.exp(alog_ref[...]) * sp
    gcum_all = _cumsum_rows(g_all)

    r2 = lax.broadcasted_iota(I32, (2 * c, 2 * c), 0)
    c2 = lax.broadcasted_iota(I32, (2 * c, 2 * c), 1)
    same = (r2 ^ c2) < c
    incl = same & (r2 >= c2)
    strict = same & (r2 > c2)
    top = lax.broadcasted_iota(I32, (2 * c, dk), 0) < c

    eye = jnp.where(r2 == c2, 1.0, 0.0)
    lane = lax.broadcasted_iota(I32, (c, LANE), 1)

    def pick(x, base, p):
        h0 = (hg * pairs + p) * 2
        a = jnp.sum(jnp.where(lane == base + h0, x, 0.0), axis=-1, keepdims=True)
        b = jnp.sum(jnp.where(lane == base + h0 + 1, x, 0.0), axis=-1, keepdims=True)
        return jnp.concatenate([jnp.broadcast_to(a, (c, dk)), jnp.broadcast_to(b, (c, dk))], axis=0)

    def bdot(a, b):
        return _dot(a.astype(BF16), b.astype(BF16))

    q = [x * lax.rsqrt(jnp.sum(x * x, axis=-1, keepdims=True) + EPS) * (dk ** -0.5) for x in q]
    k = [x * lax.rsqrt(jnp.sum(x * x, axis=-1, keepdims=True) + EPS) for x in k]
    beta = [pick(beta_all, b_col, p) for p in prs]
    gc = [pick(gcum_all, a_col, p) for p in prs]
    glast = [jnp.concatenate([jnp.broadcast_to(g[c - 1:c, :], (c, dk)),
                              jnp.broadcast_to(g[2 * c - 1:2 * c, :], (c, dk))], axis=0) for g in gc]
    eg = [jnp.exp(g) for g in gc]
    decay = [jnp.exp(jnp.where(incl, g - g.T, -jnp.inf)) for g in gc]
    kb = [k[p] * beta[p] for p in prs]
    kbf = [x.astype(BF16) for x in k]
    kk = [_dot_nt(kb[p].astype(BF16), kbf[p]) for p in prs]
    qk = [_dot_nt(q[p].astype(BF16), kbf[p]) for p in prs]
    nmat = [jnp.where(strict, -(kk[p] * decay[p]), 0.0) for p in prs]

    tinv = [eye + nm for nm in nmat]
    pw = nmat
    for _ in range(5):
        pw = [bdot(x, x) for x in pw]
        tinv = [tinv[p] + bdot(tinv[p], pw[p]) for p in prs]

    rhs = [jnp.concatenate([v[p] * beta[p], kb[p] * eg[p]], axis=1) for p in prs]
    x1 = [bdot(tinv[p], rhs[p]) for p in prs]
    resid = [rhs[p] - x1[p] + _dot3(nmat[p], x1[p]) for p in prs]
    xs = [x1[p] + bdot(tinv[p], resid[p]) for p in prs]
    u = [x[:, 0:dk] for x in xs]
    w = [x[:, dk:2 * dk] for x in xs]

    amat = [jnp.where(incl, qk[p] * decay[p], 0.0) for p in prs]
    q_dec = [q[p] * eg[p] for p in prs]
    k_dec_t = [(k[p] * jnp.exp(glast[p] - gc[p])).T.astype(BF16) for p in prs]
    cd = [jnp.exp(g) for g in glast]

    s2 = [state_ref[p] for p in prs]
    s2b = [x.astype(BF16) for x in s2]
    ws = [_dot(w[p].astype(BF16), s2b[p]) for p in prs]
    qs = [_dot(q_dec[p].astype(BF16), s2b[p]) for p in prs]
    v_new = [u[p] - jnp.where(top, ws[p][:, 0:dk], ws[p][:, dk:2 * dk]) for p in prs]
    vb = [x.astype(BF16) for x in v_new]
    o = [jnp.where(top, qs[p][:, 0:dk], qs[p][:, dk:2 * dk]) + _dot(amat[p].astype(BF16), vb[p]) for p in prs]
    zero = jnp.zeros((2 * c, dk), BF16)
    vexp = [jnp.concatenate([jnp.where(top, x, zero), jnp.where(top, zero, x)], axis=1) for x in vb]
    upd = [_dot(k_dec_t[p], vexp[p]) for p in prs]
    for p in prs:
        cdrow = jnp.concatenate([cd[p][0:1, :], cd[p][c:c + 1, :]], axis=1)
        state_ref[p] = s2[p] * cdrow + upd[p]

    for p in prs:
        on = o[p] * lax.rsqrt(jnp.mean(o[p] * o[p], axis=-1, keepdims=True) + EPS) * gn_ref[...]
        for hh in range(2):
            lo = (2 * p + hh) * dk
            zz = z_ref[:, lo:lo + dk]
            y = on[hh * c:(hh + 1) * c, :] * (zz * _sigmoid(zz))
            o_ref[:, lo:lo + dk] = y.astype(o_ref.dtype)


def gated_deltanet(proj_b, proj_small, conv_w, a_log, dt_bias, g_out, batch, seq, *, n_heads, dk,
                   small_col_block, small_row_array_cols, b_col, a_col, pairs):
    t_tok = batch * seq
    nchunk = seq // CHUNK
    width = pairs * 2 * dk
    ng = n_heads // (2 * pairs)
    hw = n_heads * dk
    sec = hw // width
    conv_width = conv_w.shape[0]
    alog_row = jnp.zeros((1, LANE), F32).at[0, a_col:a_col + n_heads].set(a_log.astype(F32))
    dtb_row = jnp.zeros((1, LANE), F32).at[0, a_col:a_col + n_heads].set(dt_bias.astype(F32))
    kern = functools.partial(_gdn_kernel, pairs=pairs, dk=dk, b_col=b_col, a_col=a_col, conv_width=conv_width)
    row = lambda b, g, n: b * nchunk + n
    return pl.pallas_call(
        kern,
        out_shape=jax.ShapeDtypeStruct((t_tok, hw), BF16),
        grid=(batch, ng, nchunk),
        in_specs=[
            pl.BlockSpec((CHUNK, width), lambda b, g, n: (row(b, g, n), g)),
            pl.BlockSpec((CHUNK, width), lambda b, g, n: (row(b, g, n), sec + g)),
            pl.BlockSpec((CHUNK, width), lambda b, g, n: (row(b, g, n), 2 * sec + g)),
            pl.BlockSpec((CHUNK, width), lambda b, g, n: (row(b, g, n), 3 * sec + g)),
            pl.BlockSpec((CHUNK, LANE), lambda b, g, n: (row(b, g, n), small_col_block)),
            pl.BlockSpec((conv_width, width), lambda b, g, n: (0, g)),
            pl.BlockSpec((conv_width, width), lambda b, g, n: (0, sec + g)),
            pl.BlockSpec((conv_width, width), lambda b, g, n: (0, 2 * sec + g)),
            pl.BlockSpec((1, LANE), lambda b, g, n: (0, 0)),
            pl.BlockSpec((1, LANE), lambda b, g, n: (0, 0)),
            pl.BlockSpec((1, dk), lambda b, g, n: (0, 0)),
        ],
        out_specs=pl.BlockSpec((CHUNK, width), lambda b, g, n: (row(b, g, n), g)),
        scratch_shapes=[pltpu.VMEM((3, CHUNK + 8, width), F32),
                        pltpu.VMEM((pairs, dk, 2 * dk), F32)],
        compiler_params=_cparams(("arbitrary", "arbitrary", "arbitrary")),
        name="gated_deltanet",
    )(proj_b, proj_b, proj_b, proj_b, proj_small, conv_w, conv_w, conv_w, alog_row, dtb_row,
      g_out.reshape(1, dk).astype(F32))


def _cross_kernel(q_ref, k_ref, v_ref, wco_ref, h_ref, g_ref, h_out_ref, n_out_ref, *, n_heads, dh):
    scale = dh ** -0.5
    outs = []
    for h in range(n_heads):
        s = _dot_nt(q_ref[:, h * dh:(h + 1) * dh], k_ref[:, h * dh:(h + 1) * dh]) * scale
        m = jnp.max(s, axis=-1, keepdims=True)
        p = jnp.exp(s - m)
        l = jnp.sum(p, axis=-1, keepdims=True)
        p = (p / l).astype(BF16)
        outs.append(_dot(p, v_ref[:, h * dh:(h + 1) * dh]).astype(BF16))
    o = jnp.concatenate(outs, axis=1)
    hn = h_ref[...] + _dot(o, wco_ref[...])
    h_out_ref[...] = hn
    ms = jnp.mean(hn * hn, axis=-1, keepdims=True)
    n_out_ref[...] = (hn * lax.rsqrt(ms + EPS) * g_ref[...]).astype(n_out_ref.dtype)


def cross_attention(qx, kvx, w_co, h, g_next, batch, seq, n_mem, *, n_heads, dh, tq=256):
    t_tok, d = h.shape
    tq = min(tq, seq)
    nq = seq // tq
    hw = n_heads * dh
    kern = functools.partial(_cross_kernel, n_heads=n_heads, dh=dh)
    return pl.pallas_call(
        kern,
        out_shape=(jax.ShapeDtypeStruct((t_tok, d), F32), jax.ShapeDtypeStruct((t_tok, d), BF16)),
        grid=(batch, nq),
        in_specs=[
            pl.BlockSpec((tq, hw), lambda b, i: (b * nq + i, 0)),
            pl.BlockSpec((n_mem, hw), lambda b, i: (b, 0)),
            pl.BlockSpec((n_mem, hw), lambda b, i: (b, 1)),
            pl.BlockSpec((hw, d), lambda b, i: (0, 0)),
            pl.BlockSpec((tq, d), lambda b, i: (b * nq + i, 0)),
            pl.BlockSpec((1, d), lambda b, i: (0, 0)),
        ],
        out_specs=(pl.BlockSpec((tq, d), lambda b, i: (b * nq + i, 0)),
                   pl.BlockSpec((tq, d), lambda b, i: (b * nq + i, 0))),
        compiler_params=_cparams(("parallel", "parallel")),
        name="cross_attention",
    )(qx, kvx, kvx, w_co, h, g_next.reshape(1, d).astype(F32))


def _pad_cols(w, n):
    return jnp.pad(w, ((0, 0), (0, n - w.shape[1])))


def kernel(x, mem, attn_norm_g, w_in, qa_norm_g, w_qb, kv_norm_g, w_uk, w_uv, w_iq, conv_w, a_log, dt_bias,
           delta_norm_g, w_o, cross_norm_g, mem_norm_g, w_cq, w_ckv, w_co, ffn_norm_g, w_ffn_in, w_ffn_out,
           final_norm_g):
    batch, seq, d = x.shape
    n_mem = mem.shape[1]
    depth = w_in.shape[0]
    q_lora = qa_norm_g.shape[1]
    kv_lora = kv_norm_g.shape[1]
    h_a, _, dh_a = w_uk.shape[1:]
    dv_a = w_uv.shape[3]
    h_idx = w_iq.shape[2] // 128
    d_idx = w_iq.shape[2] // h_idx
    h_b = a_log.shape[1]
    dk_b = delta_norm_g.shape[1]
    h_x = w_cq.shape[2] // 128
    dh_x = w_cq.shape[2] // h_x
    d_ff = w_ffn_out.shape[1]
    t_tok = batch * seq
    hw_b = h_b * dk_b

    h = x.reshape(t_tok, d)
    memf = mem.reshape(batch * n_mem, d)
    for l in range(depth):
        win = w_in[l]
        o_qa, o_ckv, o_kidx, o_widx = 0, q_lora, q_lora + kv_lora, q_lora + kv_lora + d_idx
        o_qkv = o_widx + h_idx
        o_z = o_qkv + 3 * hw_b
        o_b = o_z + hw_b
        o_a = o_b + h_b
        small = jnp.concatenate([win[:, o_widx:o_widx + h_idx], win[:, o_b:o_b + h_b], win[:, o_a:o_a + h_b]], axis=1)
        w_a = jnp.concatenate([win[:, o_qa:o_widx], _pad_cols(small, LANE)], axis=1).astype(BF16)
        w_b = win[:, o_qkv:o_qkv + 4 * hw_b].astype(BF16)
        na = w_a.shape[1]
        assert q_lora % kv_lora == 0 and (q_lora + kv_lora) % d_idx == 0 and d_idx == LANE
        kidx_cb = (q_lora + kv_lora) // d_idx
        small_cb = kidx_cb + 1
        w_q2 = jnp.concatenate([w_qb[l], w_iq[l]], axis=1).astype(BF16)
        w_uk_b = w_uk[l].astype(BF16)
        w_uv_b = w_uv[l].astype(BF16)
        w_o_b = w_o[l].astype(BF16)
        w_cq_b = w_cq[l].astype(BF16)
        w_ckv_b = w_ckv[l].astype(BF16)
        w_co_b = w_co[l].astype(BF16)
        ffp = -(-d_ff // 512) * 512
        wfi = w_ffn_in[l]
        w_fi_b = jnp.concatenate([_pad_cols(wfi[:, :d_ff], ffp), _pad_cols(wfi[:, d_ff:], ffp)], axis=1).astype(BF16)
        w_fo_b = jnp.pad(w_ffn_out[l], ((0, ffp - d_ff), (0, 0))).astype(BF16)

        n0 = rmsnorm_cols(h, attn_norm_g[l], d, 0, BF16)
        tn_a = na // 3 if na % (3 * LANE) == 0 else na
        proj_a = matmul([(n0, 0, w_a, 0, d)], na, F32, tm=1024, tn=tn_a, name="in_proj_a")
        proj_b = matmul([(n0, 0, w_b, 0, d)], 4 * hw_b, F32, tm=1024, tn=512, name="in_proj_b")
        qan = rmsnorm_cols(proj_a, qa_norm_g[l], q_lora, 0, BF16)
        ckvn = rmsnorm_cols(proj_a, kv_norm_g[l], kv_lora, q_lora // kv_lora, BF16)
        qcat = matmul([(qan, 0, w_q2, 0, q_lora)], w_q2.shape[1], BF16, tm=1024, tn=1024, name="q_proj")
        y_a = dsa_attention(qcat, proj_a, ckvn, w_uk_b, w_uv_b, batch, seq, n_heads=h_a, n_idx_heads=h_idx,
                            d_idx=d_idx, dh=dh_a, dv=dv_a, kidx_col_block=kidx_cb, small_col_block=small_cb)
        y_b = gated_deltanet(proj_b, proj_a, conv_w[l], a_log[l], dt_bias[l], delta_norm_g[l], batch, seq,
                             n_heads=h_b, dk=dk_b, small_col_block=small_cb, small_row_array_cols=na,
                             b_col=h_idx, a_col=h_idx + h_b, pairs=min(8, h_b // 2))
        k_a = h_a * dv_a
        assert k_a == hw_b
        h = matmul([(y_a, 0, w_o_b, 0, k_a), (y_b, 0, w_o_b, 1, k_a)], d, F32, tm=1024, tn=512, residual=h,
                   name="out_proj")

        memn = rmsnorm_cols(memf, mem_norm_g[l], d, 0, BF16)
        kvx = matmul([(memn, 0, w_ckv_b, 0, d)], w_ckv_b.shape[1], BF16, tm=1024, tn=512, name="mem_kv_proj")
        hn = rmsnorm_cols(h, cross_norm_g[l], d, 0, BF16)
        qx = matmul([(hn, 0, w_cq_b, 0, d)], w_cq_b.shape[1], BF16, tm=1024, tn=512, name="cross_q_proj")
        h, n2 = cross_attention(qx, kvx, w_co_b, h, ffn_norm_g[l], batch, seq, n_mem, n_heads=h_x, dh=dh_x)

        act = matmul_swiglu(n2, w_fi_b, ffp, tm=1024, tn=512)
        tk_f = ffp // 4 if ffp % (4 * LANE) == 0 else ffp
        h = matmul([(act, 0, w_fo_b, 0, ffp)], d, F32, tm=1024, tn=512, tk=tk_f, residual=h, name="ffn_out")
    out = rmsnorm_cols(h, final_norm_g, d, 0, F32)
    return out.reshape(batch, seq, d)
```

```python
import functools

import jax
import jax.numpy as jnp
from jax import lax
from jax.experimental import pallas as pl
from jax.experimental.pallas import tpu as pltpu

F32 = jnp.float32
BF16 = jnp.bfloat16
I32 = jnp.int32

EPS = 1e-6
CHUNK = 64
Q_BLOCK = 128
TOPK_MAX = 256
LANE = 128
VMEM_LIMIT = 56 * 1024 * 1024
INT_MIN = -2 ** 31
CHUNK_SHIFT = CHUNK.bit_length() - 1
assert 1 << CHUNK_SHIFT == CHUNK
NEG_BIG = -1e30


def _cparams(sem):
    return pltpu.CompilerParams(dimension_semantics=sem, vmem_limit_bytes=VMEM_LIMIT)


def _dot(a, b):
    return jnp.dot(a, b, preferred_element_type=F32)


def _dot_nt(a, b):
    return lax.dot_general(a, b, (((1,), (1,)), ((), ())), preferred_element_type=F32)


def _sigmoid(x):
    return 1.0 / (1.0 + jnp.exp(-x))


def _rmsnorm_kernel(x_ref, g_ref, o_ref):
    x = x_ref[...].astype(F32)
    ms = jnp.mean(x * x, axis=-1, keepdims=True)
    o_ref[...] = (x * lax.rsqrt(ms + EPS) * g_ref[...]).astype(o_ref.dtype)


def rmsnorm_cols(x, g, width, col_block, out_dtype, tr=256):
    m = x.shape[0]
    tr = min(tr, m)
    return pl.pallas_call(
        _rmsnorm_kernel,
        out_shape=jax.ShapeDtypeStruct((m, width), out_dtype),
        grid=(m // tr,),
        in_specs=[pl.BlockSpec((tr, width), lambda i: (i, col_block)),
                  pl.BlockSpec((1, width), lambda i: (0, 0))],
        out_specs=pl.BlockSpec((tr, width), lambda i: (i, 0)),
        compiler_params=_cparams(("parallel",)),
        name="rmsnorm",
    )(x, g.reshape(1, width).astype(F32))


def _mm_kernel(*refs, n_pairs, nk, has_res):
    a_refs = refs[0:2 * n_pairs:2]
    b_refs = refs[1:2 * n_pairs:2]
    pos = 2 * n_pairs
    r_ref = refs[pos] if has_res else None
    pos += int(has_res)
    o_ref = refs[pos]
    acc_ref = refs[pos + 1] if nk > 1 else None

    part = _dot(a_refs[0][...], b_refs[0][...])
    for a_ref, b_ref in zip(a_refs[1:], b_refs[1:]):
        part = part + _dot(a_ref[...], b_ref[...])

    def finish(acc):
        if has_res:
            acc = acc + r_ref[...]
        o_ref[...] = acc.astype(o_ref.dtype)

    if nk == 1:
        finish(part)
    else:
        k = pl.program_id(2)

        @pl.when(k == 0)
        def _():
            acc_ref[...] = part

        @pl.when(k > 0)
        def _():
            acc_ref[...] += part

        @pl.when(k == nk - 1)
        def _():
            finish(acc_ref[...])


def matmul(pairs, n, out_dtype, tm, tn, tk=None, residual=None, name="matmul"):
    m = pairs[0][0].shape[0]
    tm = min(tm, m)
    tn = min(tn, n)
    kdim = pairs[0][4]
    tk = kdim if tk is None else min(tk, kdim)
    nk = kdim // tk
    assert m % tm == 0 and n % tn == 0 and kdim % tk == 0
    in_specs, args = [], []
    for a, acb, b, brb, kd in pairs:
        assert kd == kdim
        in_specs.append(pl.BlockSpec((tm, tk), functools.partial(lambda i, j, k, o: (i, o + k), o=acb * nk)))
        in_specs.append(pl.BlockSpec((tk, tn), functools.partial(lambda i, j, k, o: (o + k, j), o=brb * nk)))
        args += [a, b]
    if residual is not None:
        in_specs.append(pl.BlockSpec((tm, tn), lambda i, j, k: (i, j)))
        args.append(residual)
    return pl.pallas_call(
        functools.partial(_mm_kernel, n_pairs=len(pairs), nk=nk, has_res=residual is not None),
        out_shape=jax.ShapeDtypeStruct((m, n), out_dtype),
        grid=(m // tm, n // tn, nk),
        in_specs=in_specs,
        out_specs=pl.BlockSpec((tm, tn), lambda i, j, k: (i, j)),
        scratch_shapes=[pltpu.VMEM((tm, tn), F32)] if nk > 1 else [],
        compiler_params=_cparams(("parallel", "parallel", "arbitrary")),
        name=name,
    )(*args)


def _swiglu_kernel(a_ref, bg_ref, bu_ref, o_ref, *, nj):
    j = pl.program_id(1)

    @pl.when(j < nj)
    def _():
        a = a_ref[...]
        gate = _dot(a, bg_ref[...])
        up = _dot(a, bu_ref[...])
        o_ref[...] = (gate * _sigmoid(gate) * up).astype(o_ref.dtype)

    @pl.when(j >= nj)
    def _():
        o_ref[...] = jnp.zeros(o_ref.shape, o_ref.dtype)


def matmul_swiglu(a, w, d_ff, d_out, tm, tn):
    m, k = a.shape
    tm = min(tm, m)
    tn = min(tn, d_ff)
    assert d_ff % tn == 0 and d_out % tn == 0
    nj = d_ff // tn
    return pl.pallas_call(
        functools.partial(_swiglu_kernel, nj=nj),
        out_shape=jax.ShapeDtypeStruct((m, d_out), BF16),
        grid=(m // tm, d_out // tn),
        in_specs=[pl.BlockSpec((tm, k), lambda i, j: (i, 0)),
                  pl.BlockSpec((k, tn), lambda i, j: (0, jnp.minimum(j, nj - 1))),
                  pl.BlockSpec((k, tn), lambda i, j: (0, jnp.minimum(j, nj - 1) + nj))],
        out_specs=pl.BlockSpec((tm, tn), lambda i, j: (i, j)),
        compiler_params=_cparams(("parallel", "parallel")),
        name="ffn_in_swiglu",
    )(a, w, w)


def _dsa_kernel(q_ref, qi_ref, sm_ref, kidx_ref, ckv_ref, wuk_ref, wuv_ref, o_ref,
                key_ref, bias_ref, st_ref, s_ref, p_ref, acc_ref, ql_ref, qis_ref, wb_ref, m_ref, l_ref, al_ref,
                *, n_heads, n_idx_heads, d_idx, dh, dv, tk, topk):
    qb = q_ref.shape[0]
    i = pl.program_id(1)
    n_keys = (i + 1) * qb
    nkt = (n_keys + tk - 1) // tk
    idx_scale = d_idx ** -0.5 * n_idx_heads ** -0.5
    att_scale = dh ** -0.5
    nlc = tk // LANE

    assert qb == LANE
    w_t = sm_ref[...].T * idx_scale
    for h in range(n_idx_heads):
        wb_ref[h] = jnp.broadcast_to(w_t[h:h + 1, :], (LANE, qb))
        qis_ref[h * qb:(h + 1) * qb, :] = qi_ref[:, h * d_idx:(h + 1) * d_idx]

    q_lane = lax.broadcasted_iota(I32, (LANE, qb), 1)
    k_row = lax.broadcasted_iota(I32, (LANE, qb), 0)
    lim = (((i * qb + q_lane) >> CHUNK_SHIFT) + 1) << CHUNK_SHIFT

    def score_tile(t, carry):
        ks = pl.multiple_of(t * tk, tk)
        kblk = kidx_ref[pl.ds(ks, tk), :].astype(BF16)
        st_ref[...] = _dot_nt(kblk, qis_ref[...])
        for c in range(nlc):
            sc = jnp.zeros((LANE, qb), F32)
            for h in range(n_idx_heads):
                sc = sc + jnp.maximum(st_ref[c * LANE:(c + 1) * LANE, h * qb:(h + 1) * qb], 0.0) * wb_ref[h]
            bits = pltpu.bitcast(sc, I32)
            key = bits ^ ((bits >> 31) & 0x7FFFFFFF)
            key = jnp.where(ks + c * LANE + k_row < lim, key, INT_MIN)
            key_ref[pl.ds(ks + c * LANE, LANE), :] = key
        return carry

    lax.fori_loop(0, nkt, score_tile, 0)

    pos_tile = lax.broadcasted_iota(I32, (tk, qb), 0)
    acc_rows = min(64, tk)

    def count(pred):
        def body(t, c):
            ks = pl.multiple_of(t * tk, tk)
            hit = jnp.where(pred(key_ref[pl.ds(ks, tk), :], ks + pos_tile), 1.0, 0.0)
            return c + jnp.sum(hit.reshape(tk // acc_rows, acc_rows, qb), axis=0)
        c = lax.fori_loop(0, nkt, body, jnp.zeros((acc_rows, qb), F32))
        return jnp.sum(c, axis=0, keepdims=True)

    kf = float(topk)
    t0 = jnp.where(count(lambda k, p: k >= 0) >= kf, 0, INT_MIN).astype(I32)

    def bisect(j, t):
        cand = t + (jnp.int32(1) << (30 - j))
        return jnp.where(count(lambda k, p: k >= cand) >= kf, cand, t)

    thr = lax.fori_loop(0, 31, bisect, t0)
    thr = jnp.maximum(thr, INT_MIN + 1)

    n_ge = count(lambda k, p: k >= thr)

    def tie_search():
        need = kf - count(lambda k, p: k > thr)

        def step(j, lo):
            cand = lo + (jnp.int32(1) << (pos_bits - 1 - j))
            below = count(lambda k, p: (k == thr) & (p < cand))
            return jnp.where(below < need, cand, lo)

        return lax.fori_loop(0, pos_bits, step, jnp.zeros((1, qb), I32))

    pos_bits = max(1, (key_ref.shape[0] - 1).bit_length())
    last = lax.cond(jnp.max(n_ge) > kf, tie_search, lambda: jnp.full((1, qb), key_ref.shape[0], I32))

    def bias_tile(t, carry):
        ks = pl.multiple_of(t * tk, tk)
        for c in range(nlc):
            k = key_ref[pl.ds(ks + c * LANE, LANE), :]
            tied = jnp.where(ks + c * LANE + k_row <= last, 0.0, NEG_BIG)
            b = jnp.where(k > thr, 0.0, jnp.where(k == thr, tied, NEG_BIG))
            bias_ref[:, pl.ds(ks + c * LANE, LANE)] = b.T
        return carry

    lax.fori_loop(0, nkt, bias_tile, 0)

    rows = n_heads * qb
    for h in range(n_heads):
        ql = _dot_nt(q_ref[:, h * dh:(h + 1) * dh], wuk_ref[h]) * att_scale
        ql_ref[h * qb:(h + 1) * qb, :] = ql.astype(BF16)
    m_ref[...] = jnp.full((rows, LANE), NEG_BIG, F32)
    l_ref[...] = jnp.zeros((rows, LANE), F32)
    acc_ref[...] = jnp.zeros(acc_ref.shape, F32)

    def attend(t, carry):
        ks = pl.multiple_of(t * tk, tk)
        ckv_t = ckv_ref[pl.ds(ks, tk), :]
        s_ref[0:rows, :] = _dot_nt(ql_ref[...], ckv_t)
        for h in range(n_heads):
            r0 = h * qb
            s = [s_ref[r0:r0 + qb, c * LANE:(c + 1) * LANE] + bias_ref[:, pl.ds(ks + c * LANE, LANE)]
                 for c in range(nlc)]
            tmax = s[0]
            for c in range(1, nlc):
                tmax = jnp.maximum(tmax, s[c])
            m_old = m_ref[r0:r0 + qb, :]
            m_new = jnp.maximum(m_old, jnp.max(tmax, axis=-1, keepdims=True))
            alpha = jnp.exp(m_old - m_new)
            lsum = l_ref[r0:r0 + qb, :] * alpha
            for c in range(nlc):
                p = jnp.exp(s[c] - m_new)
                lsum = lsum + p
                p_ref[r0:r0 + qb, c * LANE:(c + 1) * LANE] = p.astype(BF16)
            m_ref[r0:r0 + qb, :] = m_new
            l_ref[r0:r0 + qb, :] = lsum
            al_ref[r0:r0 + qb, :] = alpha
        s_ref[0:rows, 0:ckv_t.shape[1]] = _dot(p_ref[...], ckv_t)
        for h in range(n_heads):
            r0 = h * qb
            alpha = al_ref[r0:r0 + qb, :]
            for c in range(ckv_t.shape[1] // LANE):
                cs = slice(c * LANE, (c + 1) * LANE)
                acc_ref[r0:r0 + qb, cs] = acc_ref[r0:r0 + qb, cs] * alpha + s_ref[r0:r0 + qb, cs]
        return carry

    lax.fori_loop(0, nkt, attend, 0)
    for h in range(n_heads):
        r0 = h * qb
        lrow = jnp.sum(l_ref[r0:r0 + qb, :], axis=-1, keepdims=True)
        o_lat = (acc_ref[r0:r0 + qb, :] / lrow).astype(BF16)
        o_ref[:, h * dv:(h + 1) * dv] = _dot(o_lat, wuv_ref[h]).astype(o_ref.dtype)


def dsa_attention(qcat, proj_a, ckvn, w_uk, w_uv, batch, seq, *, n_heads, n_idx_heads, d_idx, dh, dv,
                  kidx_col_block, small_col_block):
    t_tok = batch * seq
    kv_lora = ckvn.shape[1]
    qb = min(Q_BLOCK, seq)
    nb = seq // qb
    tk = min(512, seq)
    topk = min(TOPK_MAX, seq // 4)
    qw = n_heads * dh
    qiw = n_idx_heads * d_idx
    assert qw == qiw and kv_lora <= tk
    kern = functools.partial(_dsa_kernel, n_heads=n_heads, n_idx_heads=n_idx_heads, d_idx=d_idx, dh=dh, dv=dv,
                             tk=tk, topk=topk)
    once = pl.Buffered(1)
    return pl.pallas_call(
        kern,
        out_shape=jax.ShapeDtypeStruct((t_tok, n_heads * dv), BF16),
        grid=(batch, nb),
        in_specs=[
            pl.BlockSpec((qb, qw), lambda b, i: (b * nb + i, 0)),
            pl.BlockSpec((qb, qiw), lambda b, i: (b * nb + i, 1)),
            pl.BlockSpec((qb, LANE), lambda b, i: (b * nb + i, small_col_block)),
            pl.BlockSpec((seq, d_idx), lambda b, i: (b, kidx_col_block), pipeline_mode=once),
            pl.BlockSpec((seq, kv_lora), lambda b, i: (b, 0), pipeline_mode=once),
            pl.BlockSpec(w_uk.shape, lambda b, i: (0, 0, 0), pipeline_mode=once),
            pl.BlockSpec(w_uv.shape, lambda b, i: (0, 0, 0), pipeline_mode=once),
        ],
        out_specs=pl.BlockSpec((qb, n_heads * dv), lambda b, i: (b * nb + i, 0)),
        scratch_shapes=[
            pltpu.VMEM((seq, qb), I32),
            pltpu.VMEM((qb, seq), F32),
            pltpu.VMEM((tk, n_idx_heads * qb), F32),
            pltpu.VMEM((n_heads * qb, tk), F32),
            pltpu.VMEM((n_heads * qb, tk), BF16),
            pltpu.VMEM((n_heads * qb, kv_lora), F32),
            pltpu.VMEM((n_heads * qb, kv_lora), BF16),
            pltpu.VMEM((n_idx_heads * qb, d_idx), BF16),
            pltpu.VMEM((n_idx_heads, LANE, qb), F32),
            pltpu.VMEM((n_heads * qb, LANE), F32),
            pltpu.VMEM((n_heads * qb, LANE), F32),
            pltpu.VMEM((n_heads * qb, LANE), F32),
        ],
        compiler_params=_cparams(("arbitrary", "arbitrary")),
        name="dsa_attention",
    )(qcat, qcat, proj_a, proj_a, ckvn, w_uk, w_uv)


def _split_bf16(x):
    hi = x.astype(BF16)
    lo = (x - hi.astype(F32)).astype(BF16)
    return hi, lo


def _dot3(a, b):
    ah, al = _split_bf16(a)
    bh, bl = _split_bf16(b)
    lhs = jnp.concatenate([ah, ah, al], axis=1)
    rhs = jnp.concatenate([bh, bl, bh], axis=0)
    return _dot(lhs, rhs)


def _cumsum_rows(x):
    n = x.shape[0]
    row = lax.broadcasted_iota(I32, x.shape, 0)
    s = 1
    while s < n:
        x = x + jnp.where(row >= s, pltpu.roll(x, s, 0), 0.0)
        s *= 2
    return x


def _gdn_kernel(xq_ref, xk_ref, xv_ref, z_ref, gate_ref, cwq_ref, cwk_ref, cwv_ref, alog_ref, dtb_ref, gn_ref,
                o_ref, ext_ref, state_ref, *, pairs, dk, b_col, a_col, conv_width):
    c = CHUNK
    assert dk == 2 * c
    n = pl.program_id(2)
    hg = pl.program_id(1)
    width = pairs * 2 * dk
    halo = 8

    @pl.when(n == 0)
    def _():
        ext_ref[:, 0:halo, :] = jnp.zeros((3, halo, width), F32)
        state_ref[...] = jnp.zeros(state_ref.shape, F32)

    prs = range(pairs)

    def conv_silu(idx, cw_ref, p):
        halves = []
        for hh in range(2):
            lo = (2 * p + hh) * dk
            y = jnp.zeros((c, dk), F32)
            for j in range(conv_width):
                off = halo - (conv_width - 1) + j
                y = y + ext_ref[idx, off:off + c, lo:lo + dk] * cw_ref[j:j + 1, lo:lo + dk]
            halves.append(y * _sigmoid(y))
        return jnp.concatenate(halves, axis=0)

    for idx, x_ref in enumerate((xq_ref, xk_ref, xv_ref)):
        ext_ref[idx, halo:halo + c, :] = x_ref[...]
    q = [conv_silu(0, cwq_ref, p) for p in prs]
    k = [conv_silu(1, cwk_ref, p) for p in prs]
    v = [conv_silu(2, cwv_ref, p) for p in prs]
    for idx in range(3):
        ext_ref[idx, 0:halo, :] = ext_ref[idx, c:c + halo, :]

    gates = gate_ref[...]
    beta_all = _sigmoid(gates)
    sp = jnp.maximum(gates + dtb_ref[...], 0.0) + jnp.log(1.0 + jnp.exp(-jnp.abs(gates + dtb_ref[...])))
    g_all = -jnp.exp(alog_ref[...]) * sp
    gcum_all = _cumsum_rows(g_all)

    r2 = lax.broadcasted_iota(I32, (2 * c, 2 * c), 0)
    c2 = lax.broadcasted_iota(I32, (2 * c, 2 * c), 1)
    same = (r2 ^ c2) < c
    incl = same & (r2 >= c2)
    strict = same & (r2 > c2)
    top = lax.broadcasted_iota(I32, (2 * c, dk), 0) < c

    eye = jnp.where(r2 == c2, 1.0, 0.0)
    lane = lax.broadcasted_iota(I32, (c, LANE), 1)

    def pick(x, base, p):
        h0 = (hg * pairs + p) * 2
        a = jnp.sum(jnp.where(lane == base + h0, x, 0.0), axis=-1, keepdims=True)
        b = jnp.sum(jnp.where(lane == base + h0 + 1, x, 0.0), axis=-1, keepdims=True)
        return jnp.concatenate([jnp.broadcast_to(a, (c, dk)), jnp.broadcast_to(b, (c, dk))], axis=0)

    def bdot(a, b):
        return _dot(a.astype(BF16), b.astype(BF16))

    q = [x * lax.rsqrt(jnp.sum(x * x, axis=-1, keepdims=True) + EPS) * (dk ** -0.5) for x in q]
    k = [x * lax.rsqrt(jnp.sum(x * x, axis=-1, keepdims=True) + EPS) for x in k]
    beta = [pick(beta_all, b_col, p) for p in prs]
    gc = [pick(gcum_all, a_col, p) for p in prs]
    glast = [jnp.concatenate([jnp.broadcast_to(g[c - 1:c, :], (c, dk)),
                              jnp.broadcast_to(g[2 * c - 1:2 * c, :], (c, dk))], axis=0) for g in gc]
    eg = [jnp.exp(g) for g in gc]
    decay = [jnp.exp(jnp.where(incl, g - g.T, -jnp.inf)) for g in gc]
    kb = [k[p] * beta[p] for p in prs]
    kbf = [x.astype(BF16) for x in k]
    kk = [_dot_nt(kb[p].astype(BF16), kbf[p]) for p in prs]
    qk = [_dot_nt(q[p].astype(BF16), kbf[p]) for p in prs]
    nmat = [jnp.where(strict, -(kk[p] * decay[p]), 0.0) for p in prs]

    tinv = [eye + nm for nm in nmat]
    pw = nmat
    for _ in range(5):
        pw = [bdot(x, x) for x in pw]
        tinv = [tinv[p] + bdot(tinv[p], pw[p]) for p in prs]

    rhs = [jnp.concatenate([v[p] * beta[p], kb[p] * eg[p]], axis=1) for p in prs]
    x1 = [bdot(tinv[p], rhs[p]) for p in prs]
    resid = [rhs[p] - x1[p] + _dot3(nmat[p], x1[p]) for p in prs]
    xs = [x1[p] + bdot(tinv[p], resid[p]) for p in prs]
    u = [x[:, 0:dk] for x in xs]
    w = [x[:, dk:2 * dk] for x in xs]

    amat = [jnp.where(incl, qk[p] * decay[p], 0.0) for p in prs]
    q_dec = [q[p] * eg[p] for p in prs]
    k_dec_t = [(k[p] * jnp.exp(glast[p] - gc[p])).T.astype(BF16) for p in prs]
    cd = [jnp.exp(g) for g in glast]

    s2 = [state_ref[p] for p in prs]
    s2b = [x.astype(BF16) for x in s2]
    ws = [_dot(w[p].astype(BF16), s2b[p]) for p in prs]
    qs = [_dot(q_dec[p].astype(BF16), s2b[p]) for p in prs]
    v_new = [u[p] - jnp.where(top, ws[p][:, 0:dk], ws[p][:, dk:2 * dk]) for p in prs]
    vb = [x.astype(BF16) for x in v_new]
    o = [jnp.where(top, qs[p][:, 0:dk], qs[p][:, dk:2 * dk]) + _dot(amat[p].astype(BF16), vb[p]) for p in prs]
    zero = jnp.zeros((2 * c, dk), BF16)
    vexp = [jnp.concatenate([jnp.where(top, x, zero), jnp.where(top, zero, x)], axis=1) for x in vb]
    upd = [_dot(k_dec_t[p], vexp[p]) for p in prs]
    for p in prs:
        cdrow = jnp.concatenate([cd[p][0:1, :], cd[p][c:c + 1, :]], axis=1)
        state_ref[p] = s2[p] * cdrow + upd[p]

    for p in prs:
        on = o[p] * lax.rsqrt(jnp.mean(o[p] * o[p], axis=-1, keepdims=True) + EPS) * gn_ref[...]
        for hh in range(2):
            lo = (2 * p + hh) * dk
            zz = z_ref[:, lo:lo + dk]
            y = on[hh * c:(hh + 1) * c, :] * (zz * _sigmoid(zz))
            o_ref[:, lo:lo + dk] = y.astype(o_ref.dtype)


def gated_deltanet(proj_b, proj_small, conv_w, a_log, dt_bias, g_out, batch, seq, *, n_heads, dk,
                   small_col_block, small_row_array_cols, b_col, a_col, pairs):
    t_tok = batch * seq
    nchunk = seq // CHUNK
    width = pairs * 2 * dk
    ng = n_heads // (2 * pairs)
    hw = n_heads * dk
    sec = hw // width
    conv_width = conv_w.shape[0]
    alog_row = jnp.zeros((1, LANE), F32).at[0, a_col:a_col + n_heads].set(a_log.astype(F32))
    dtb_row = jnp.zeros((1, LANE), F32).at[0, a_col:a_col + n_heads].set(dt_bias.astype(F32))
    kern = functools.partial(_gdn_kernel, pairs=pairs, dk=dk, b_col=b_col, a_col=a_col, conv_width=conv_width)
    row = lambda b, g, n: b * nchunk + n
    return pl.pallas_call(
        kern,
        out_shape=jax.ShapeDtypeStruct((t_tok, hw), BF16),
        grid=(batch, ng, nchunk),
        in_specs=[
            pl.BlockSpec((CHUNK, width), lambda b, g, n: (row(b, g, n), g)),
            pl.BlockSpec((CHUNK, width), lambda b, g, n: (row(b, g, n), sec + g)),
            pl.BlockSpec((CHUNK, width), lambda b, g, n: (row(b, g, n), 2 * sec + g)),
            pl.BlockSpec((CHUNK, width), lambda b, g, n: (row(b, g, n), 3 * sec + g)),
            pl.BlockSpec((CHUNK, LANE), lambda b, g, n: (row(b, g, n), small_col_block)),
            pl.BlockSpec((conv_width, width), lambda b, g, n: (0, g)),
            pl.BlockSpec((conv_width, width), lambda b, g, n: (0, sec + g)),
            pl.BlockSpec((conv_width, width), lambda b, g, n: (0, 2 * sec + g)),
            pl.BlockSpec((1, LANE), lambda b, g, n: (0, 0)),
            pl.BlockSpec((1, LANE), lambda b, g, n: (0, 0)),
            pl.BlockSpec((1, dk), lambda b, g, n: (0, 0)),
        ],
        out_specs=pl.BlockSpec((CHUNK, width), lambda b, g, n: (row(b, g, n), g)),
        scratch_shapes=[pltpu.VMEM((3, CHUNK + 8, width), F32),
                        pltpu.VMEM((pairs, dk, 2 * dk), F32)],
        compiler_params=_cparams(("arbitrary", "arbitrary", "arbitrary")),
        name="gated_deltanet",
    )(proj_b, proj_b, proj_b, proj_b, proj_small, conv_w, conv_w, conv_w, alog_row, dtb_row,
      g_out.reshape(1, dk).astype(F32))


def _cross_kernel(h_ref, gq_ref, wcq_ref, k_ref, v_ref, wco_ref, g_ref, h_out_ref, n_out_ref, *, n_heads, dh):
    scale = dh ** -0.5
    h_in = h_ref[...]
    ms = jnp.mean(h_in * h_in, axis=-1, keepdims=True)
    hq = (h_in * lax.rsqrt(ms + EPS) * gq_ref[...]).astype(BF16)
    q = _dot(hq, wcq_ref[...]).astype(BF16)
    outs = []
    for h in range(n_heads):
        s = _dot_nt(q[:, h * dh:(h + 1) * dh], k_ref[:, h * dh:(h + 1) * dh]) * scale
        m = jnp.max(s, axis=-1, keepdims=True)
        p = jnp.exp(s - m)
        l = jnp.sum(p, axis=-1, keepdims=True)
        p = (p / l).astype(BF16)
        outs.append(_dot(p, v_ref[:, h * dh:(h + 1) * dh]).astype(BF16))
    o = jnp.concatenate(outs, axis=1)
    hn = h_in + _dot(o, wco_ref[...])
    h_out_ref[...] = hn
    ms = jnp.mean(hn * hn, axis=-1, keepdims=True)
    n_out_ref[...] = (hn * lax.rsqrt(ms + EPS) * g_ref[...]).astype(n_out_ref.dtype)


def cross_attention(h, g_q, w_cq, kvx, w_co, g_next, batch, seq, n_mem, *, n_heads, dh, tq=256):
    t_tok, d = h.shape
    tq = min(tq, seq)
    nq = seq // tq
    hw = n_heads * dh
    kern = functools.partial(_cross_kernel, n_heads=n_heads, dh=dh)
    once = pl.Buffered(1)
    return pl.pallas_call(
        kern,
        out_shape=(jax.ShapeDtypeStruct((t_tok, d), F32), jax.ShapeDtypeStruct((t_tok, d), BF16)),
        grid=(batch, nq),
        in_specs=[
            pl.BlockSpec((tq, d), lambda b, i: (b * nq + i, 0)),
            pl.BlockSpec((1, d), lambda b, i: (0, 0)),
            pl.BlockSpec((d, hw), lambda b, i: (0, 0), pipeline_mode=once),
            pl.BlockSpec((n_mem, hw), lambda b, i: (b, 0)),
            pl.BlockSpec((n_mem, hw), lambda b, i: (b, 1)),
            pl.BlockSpec((hw, d), lambda b, i: (0, 0), pipeline_mode=once),
            pl.BlockSpec((1, d), lambda b, i: (0, 0)),
        ],
        out_specs=(pl.BlockSpec((tq, d), lambda b, i: (b * nq + i, 0)),
                   pl.BlockSpec((tq, d), lambda b, i: (b * nq + i, 0))),
        compiler_params=_cparams(("parallel", "parallel")),
        name="cross_attention",
    )(h, g_q.reshape(1, d).astype(F32), w_cq, kvx, kvx, w_co, g_next.reshape(1, d).astype(F32))


def _pad_cols(w, n):
    return jnp.pad(w, ((0, 0), (0, n - w.shape[1])))


def kernel(x, mem, attn_norm_g, w_in, qa_norm_g, w_qb, kv_norm_g, w_uk, w_uv, w_iq, conv_w, a_log, dt_bias,
           delta_norm_g, w_o, cross_norm_g, mem_norm_g, w_cq, w_ckv, w_co, ffn_norm_g, w_ffn_in, w_ffn_out,
           final_norm_g):
    batch, seq, d = x.shape
    n_mem = mem.shape[1]
    depth = w_in.shape[0]
    q_lora = qa_norm_g.shape[1]
    kv_lora = kv_norm_g.shape[1]
    h_a, _, dh_a = w_uk.shape[1:]
    dv_a = w_uv.shape[3]
    h_idx = w_iq.shape[2] // 128
    d_idx = w_iq.shape[2] // h_idx
    h_b = a_log.shape[1]
    dk_b = delta_norm_g.shape[1]
    h_x = w_cq.shape[2] // 128
    dh_x = w_cq.shape[2] // h_x
    d_ff = w_ffn_out.shape[1]
    t_tok = batch * seq
    hw_b = h_b * dk_b

    h = x.reshape(t_tok, d)
    memf = mem.reshape(batch * n_mem, d)
    for l in range(depth):
        win = w_in[l]
        o_qa, o_ckv, o_kidx, o_widx = 0, q_lora, q_lora + kv_lora, q_lora + kv_lora + d_idx
        o_qkv = o_widx + h_idx
        o_z = o_qkv + 3 * hw_b
        o_b = o_z + hw_b
        o_a = o_b + h_b
        small = jnp.concatenate([win[:, o_widx:o_widx + h_idx], win[:, o_b:o_b + h_b], win[:, o_a:o_a + h_b]], axis=1)
        w_a = jnp.concatenate([win[:, o_qa:o_widx], _pad_cols(small, LANE)], axis=1).astype(BF16)
        w_b = win[:, o_qkv:o_qkv + 4 * hw_b].astype(BF16)
        na = w_a.shape[1]
        assert q_lora % kv_lora == 0 and (q_lora + kv_lora) % d_idx == 0 and d_idx == LANE
        kidx_cb = (q_lora + kv_lora) // d_idx
        small_cb = kidx_cb + 1
        w_q2 = jnp.concatenate([w_qb[l], w_iq[l]], axis=1).astype(BF16)
        w_uk_b = w_uk[l].astype(BF16)
        w_uv_b = w_uv[l].astype(BF16)
        w_o_b = w_o[l].astype(BF16)
        w_cq_b = w_cq[l].astype(BF16)
        w_ckv_b = w_ckv[l].astype(BF16)
        w_co_b = w_co[l].astype(BF16)
        ffp = -(-d_ff // 1024) * 1024
        w_fi_b = w_ffn_in[l].astype(BF16)
        w_fo_b = jnp.pad(w_ffn_out[l], ((0, ffp - d_ff), (0, 0))).astype(BF16)

        n0 = rmsnorm_cols(h, attn_norm_g[l], d, 0, BF16)
        tn_a = na // 3 if na % (3 * LANE) == 0 else na
        proj_a = matmul([(n0, 0, w_a, 0, d)], na, F32, tm=1024, tn=tn_a, name="in_proj_a")
        proj_b = matmul([(n0, 0, w_b, 0, d)], 4 * hw_b, F32, tm=1024, tn=512, name="in_proj_b")
        qan = rmsnorm_cols(proj_a, qa_norm_g[l], q_lora, 0, BF16)
        ckvn = rmsnorm_cols(proj_a, kv_norm_g[l], kv_lora, q_lora // kv_lora, BF16)
        qcat = matmul([(qan, 0, w_q2, 0, q_lora)], w_q2.shape[1], BF16, tm=1024, tn=1024, name="q_proj")
        y_a = dsa_attention(qcat, proj_a, ckvn, w_uk_b, w_uv_b, batch, seq, n_heads=h_a, n_idx_heads=h_idx,
                            d_idx=d_idx, dh=dh_a, dv=dv_a, kidx_col_block=kidx_cb, small_col_block=small_cb)
        y_b = gated_deltanet(proj_b, proj_a, conv_w[l], a_log[l], dt_bias[l], delta_norm_g[l], batch, seq,
                             n_heads=h_b, dk=dk_b, small_col_block=small_cb, small_row_array_cols=na,
                             b_col=h_idx, a_col=h_idx + h_b, pairs=min(8, h_b // 2))
        k_a = h_a * dv_a
        assert k_a == hw_b
        h = matmul([(y_a, 0, w_o_b, 0, k_a), (y_b, 0, w_o_b, 1, k_a)], d, F32, tm=1024, tn=512, residual=h,
                   name="out_proj")

        memn = rmsnorm_cols(memf, mem_norm_g[l], d, 0, BF16)
        kvx = matmul([(memn, 0, w_ckv_b, 0, d)], w_ckv_b.shape[1], BF16, tm=1024, tn=512, name="mem_kv_proj")
        h, n2 = cross_attention(h, cross_norm_g[l], w_cq_b, kvx, w_co_b, ffn_norm_g[l], batch, seq, n_mem,
                                n_heads=h_x, dh=dh_x)

        act = matmul_swiglu(n2, w_fi_b, d_ff, ffp, tm=2048, tn=256)
        h = matmul([(act, 0, w_fo_b, 0, ffp)], d, F32, tm=1024, tn=1024, tk=ffp // 4, residual=h, name="ffn_out")
    out = rmsnorm_cols(h, final_norm_g, d, 0, F32)
    return out.reshape(batch, seq, d)
```

```python
import functools

import jax
import jax.numpy as jnp
from jax import lax
from jax.experimental import pallas as pl
from jax.experimental.pallas import tpu as pltpu

F32 = jnp.float32
BF16 = jnp.bfloat16
I32 = jnp.int32

EPS = 1e-6
CHUNK = 64
Q_BLOCK = 128
TOPK_MAX = 256
LANE = 128
VMEM_LIMIT = 56 * 1024 * 1024
INT_MIN = -2 ** 31
CHUNK_SHIFT = CHUNK.bit_length() - 1
assert 1 << CHUNK_SHIFT == CHUNK
NEG_BIG = -1e30


def _cparams(sem):
    return pltpu.CompilerParams(dimension_semantics=sem, vmem_limit_bytes=VMEM_LIMIT)


def _dot(a, b):
    return jnp.dot(a, b, preferred_element_type=F32)


def _dot_nt(a, b):
    return lax.dot_general(a, b, (((1,), (1,)), ((), ())), preferred_element_type=F32)


def _sigmoid(x):
    return 1.0 / (1.0 + jnp.exp(-x))


def _rmsnorm_kernel(x_ref, g_ref, o_ref):
    x = x_ref[...].astype(F32)
    ms = jnp.mean(x * x, axis=-1, keepdims=True)
    o_ref[...] = (x * lax.rsqrt(ms + EPS) * g_ref[...]).astype(o_ref.dtype)


def rmsnorm_cols(x, g, width, col_block, out_dtype, tr=256):
    m = x.shape[0]
    tr = min(tr, m)
    return pl.pallas_call(
        _rmsnorm_kernel,
        out_shape=jax.ShapeDtypeStruct((m, width), out_dtype),
        grid=(m // tr,),
        in_specs=[pl.BlockSpec((tr, width), lambda i: (i, col_block)),
                  pl.BlockSpec((1, width), lambda i: (0, 0))],
        out_specs=pl.BlockSpec((tr, width), lambda i: (i, 0)),
        compiler_params=_cparams(("parallel",)),
        name="rmsnorm",
    )(x, g.reshape(1, width).astype(F32))


def _mm_kernel(*refs, n_pairs, nk, has_res):
    a_refs = refs[0:2 * n_pairs:2]
    b_refs = refs[1:2 * n_pairs:2]
    pos = 2 * n_pairs
    r_ref = refs[pos] if has_res else None
    pos += int(has_res)
    o_ref = refs[pos]
    acc_ref = refs[pos + 1] if nk > 1 else None

    part = _dot(a_refs[0][...], b_refs[0][...].astype(BF16))
    for a_ref, b_ref in zip(a_refs[1:], b_refs[1:]):
        part = part + _dot(a_ref[...], b_ref[...].astype(BF16))

    def finish(acc):
        if has_res:
            acc = acc + r_ref[...]
        o_ref[...] = acc.astype(o_ref.dtype)

    if nk == 1:
        finish(part)
    else:
        k = pl.program_id(2)

        @pl.when(k == 0)
        def _():
            acc_ref[...] = part

        @pl.when(k > 0)
        def _():
            acc_ref[...] += part

        @pl.when(k == nk - 1)
        def _():
            finish(acc_ref[...])


def matmul(pairs, n, out_dtype, tm, tn, tk=None, residual=None, name="matmul"):
    m = pairs[0][0].shape[0]
    tm = min(tm, m)
    tn = min(tn, n)
    kdim = pairs[0][4]
    tk = kdim if tk is None else min(tk, kdim)
    nk = kdim // tk
    assert m % tm == 0 and n % tn == 0 and kdim % tk == 0
    in_specs, args = [], []
    for a, acb, b, brb, kd in pairs:
        assert kd == kdim
        in_specs.append(pl.BlockSpec((tm, tk), functools.partial(lambda i, j, k, o: (i, o + k), o=acb * nk)))
        in_specs.append(pl.BlockSpec((tk, tn), functools.partial(lambda i, j, k, o: (o + k, j), o=brb * nk)))
        args += [a, b]
    if residual is not None:
        in_specs.append(pl.BlockSpec((tm, tn), lambda i, j, k: (i, j)))
        args.append(residual)
    return pl.pallas_call(
        functools.partial(_mm_kernel, n_pairs=len(pairs), nk=nk, has_res=residual is not None),
        out_shape=jax.ShapeDtypeStruct((m, n), out_dtype),
        grid=(m // tm, n // tn, nk),
        in_specs=in_specs,
        out_specs=pl.BlockSpec((tm, tn), lambda i, j, k: (i, j)),
        scratch_shapes=[pltpu.VMEM((tm, tn), F32)] if nk > 1 else [],
        compiler_params=_cparams(("parallel", "parallel", "arbitrary")),
        name=name,
    )(*args)


def _swiglu_kernel(a_ref, bg_ref, bu_ref, o_ref, *, nj):
    j = pl.program_id(1)

    @pl.when(j < nj)
    def _():
        a = a_ref[...]
        gate = _dot(a, bg_ref[...].astype(BF16))
        up = _dot(a, bu_ref[...].astype(BF16))
        o_ref[...] = (gate * _sigmoid(gate) * up).astype(o_ref.dtype)

    @pl.when(j >= nj)
    def _():
        o_ref[...] = jnp.zeros(o_ref.shape, o_ref.dtype)


def matmul_swiglu(a, w, d_ff, d_out, tm, tn):
    m, k = a.shape
    tm = min(tm, m)
    tn = min(tn, d_ff)
    assert d_ff % tn == 0 and d_out % tn == 0
    nj = d_ff // tn
    return pl.pallas_call(
        functools.partial(_swiglu_kernel, nj=nj),
        out_shape=jax.ShapeDtypeStruct((m, d_out), BF16),
        grid=(m // tm, d_out // tn),
        in_specs=[pl.BlockSpec((tm, k), lambda i, j: (i, 0), pipeline_mode=pl.Buffered(1)),
                  pl.BlockSpec((k, tn), lambda i, j: (0, jnp.minimum(j, nj - 1))),
                  pl.BlockSpec((k, tn), lambda i, j: (0, jnp.minimum(j, nj - 1) + nj))],
        out_specs=pl.BlockSpec((tm, tn), lambda i, j: (i, j)),
        compiler_params=_cparams(("parallel", "parallel")),
        name="ffn_in_swiglu",
    )(a, w, w)


def _dsa_kernel(q_ref, qi_ref, sm_ref, kidx_ref, ckv_ref, wuk_ref, wuv_ref, o_ref,
                key_ref, bias_ref, st_ref, s_ref, p_ref, pv_ref, acc_ref, ql_ref, qis_ref, wb_ref, m_ref, l_ref, al_ref,
                *, n_heads, n_idx_heads, d_idx, dh, dv, tk, topk):
    qb = q_ref.shape[0]
    i = pl.program_id(1)
    n_keys = (i + 1) * qb
    nkt = (n_keys + tk - 1) // tk
    idx_scale = d_idx ** -0.5 * n_idx_heads ** -0.5
    att_scale = dh ** -0.5
    nlc = tk // LANE

    assert qb == LANE
    w_t = sm_ref[...].T * idx_scale
    for h in range(n_idx_heads):
        wb_ref[h] = jnp.broadcast_to(w_t[h:h + 1, :], (LANE, qb))
        qis_ref[h * qb:(h + 1) * qb, :] = qi_ref[:, h * d_idx:(h + 1) * d_idx]

    q_lane = lax.broadcasted_iota(I32, (LANE, qb), 1)
    k_row = lax.broadcasted_iota(I32, (LANE, qb), 0)
    lim = (((i * qb + q_lane) >> CHUNK_SHIFT) + 1) << CHUNK_SHIFT

    def score_tile(t, carry):
        ks = pl.multiple_of(t * tk, tk)
        kblk = kidx_ref[pl.ds(ks, tk), :].astype(BF16)
        st_ref[...] = _dot_nt(kblk, qis_ref[...])
        for c in range(nlc):
            sc = jnp.zeros((LANE, qb), F32)
            for h in range(n_idx_heads):
                sc = sc + jnp.maximum(st_ref[c * LANE:(c + 1) * LANE, h * qb:(h + 1) * qb], 0.0) * wb_ref[h]
            bits = pltpu.bitcast(sc, I32)
            key = bits ^ ((bits >> 31) & 0x7FFFFFFF)
            key = jnp.where(ks + c * LANE + k_row < lim, key, INT_MIN)
            key_ref[pl.ds(ks + c * LANE, LANE), :] = key
        return carry

    lax.fori_loop(0, nkt, score_tile, 0)

    pos_tile = lax.broadcasted_iota(I32, (tk, qb), 0)
    acc_rows = min(64, tk)

    def count(pred):
        def body(t, c):
            ks = pl.multiple_of(t * tk, tk)
            hit = jnp.where(pred(key_ref[pl.ds(ks, tk), :], ks + pos_tile), 1.0, 0.0)
            return c + jnp.sum(hit.reshape(tk // acc_rows, acc_rows, qb), axis=0)
        c = lax.fori_loop(0, nkt, body, jnp.zeros((acc_rows, qb), F32))
        return jnp.sum(c, axis=0, keepdims=True)

    kf = float(topk)
    t0 = jnp.where(count(lambda k, p: k >= 0) >= kf, 0, INT_MIN).astype(I32)

    def bisect(j, t):
        cand = t + (jnp.int32(1) << (30 - j))
        return jnp.where(count(lambda k, p: k >= cand) >= kf, cand, t)

    thr = lax.fori_loop(0, 31, bisect, t0)
    thr = jnp.maximum(thr, INT_MIN + 1)

    n_ge = count(lambda k, p: k >= thr)

    def tie_search():
        need = kf - count(lambda k, p: k > thr)

        def step(j, lo):
            cand = lo + (jnp.int32(1) << (pos_bits - 1 - j))
            below = count(lambda k, p: (k == thr) & (p < cand))
            return jnp.where(below < need, cand, lo)

        return lax.fori_loop(0, pos_bits, step, jnp.zeros((1, qb), I32))

    pos_bits = max(1, (key_ref.shape[0] - 1).bit_length())
    last = lax.cond(jnp.max(n_ge) > kf, tie_search, lambda: jnp.full((1, qb), key_ref.shape[0], I32))

    def bias_tile(t, carry):
        ks = pl.multiple_of(t * tk, tk)
        for c in range(nlc):
            k = key_ref[pl.ds(ks + c * LANE, LANE), :]
            tied = jnp.where(ks + c * LANE + k_row <= last, 0.0, NEG_BIG)
            b = jnp.where(k > thr, 0.0, jnp.where(k == thr, tied, NEG_BIG))
            bias_ref[:, pl.ds(ks + c * LANE, LANE)] = b.T
        return carry

    lax.fori_loop(0, nkt, bias_tile, 0)

    rows = n_heads * qb
    for h in range(n_heads):
        ql = _dot_nt(q_ref[:, h * dh:(h + 1) * dh], wuk_ref[h]) * att_scale
        ql_ref[h * qb:(h + 1) * qb, :] = ql.astype(BF16)
    m_ref[...] = jnp.full((rows, LANE), NEG_BIG, F32)
    l_ref[...] = jnp.zeros((rows, LANE), F32)
    acc_ref[...] = jnp.zeros(acc_ref.shape, F32)

    hpg = 4 if n_heads % 4 == 0 else 1
    grows = hpg * qb

    def attend(t, carry):
        ks = pl.multiple_of(t * tk, tk)
        ckv_t = ckv_ref[pl.ds(ks, tk), :]
        for g in range(n_heads // hpg):
            g0 = g * grows
            s_ref[g0:g0 + grows, :] = _dot_nt(ql_ref[g0:g0 + grows, :], ckv_t)
        for g in range(n_heads // hpg):
            g0 = g * grows
            for h in range(g * hpg, (g + 1) * hpg):
                r0 = h * qb

                def masked(c):
                    return s_ref[r0:r0 + qb, c * LANE:(c + 1) * LANE] + bias_ref[:, pl.ds(ks + c * LANE, LANE)]

                tmax = masked(0)
                for c in range(1, nlc):
                    tmax = jnp.maximum(tmax, masked(c))
                m_old = m_ref[r0:r0 + qb, :]
                m_new = jnp.maximum(m_old, jnp.max(tmax, axis=-1, keepdims=True))
                alpha = jnp.exp(m_old - m_new)
                lsum = l_ref[r0:r0 + qb, :] * alpha
                for c in range(nlc):
                    p = jnp.exp(masked(c) - m_new)
                    lsum = lsum + p
                    p_ref[r0:r0 + qb, c * LANE:(c + 1) * LANE] = p.astype(BF16)
                m_ref[r0:r0 + qb, :] = m_new
                l_ref[r0:r0 + qb, :] = lsum
                al_ref[r0:r0 + qb, :] = alpha
            pv_ref[g0:g0 + grows, :] = _dot(p_ref[g0:g0 + grows, :], ckv_t)
            for h in range(g * hpg, (g + 1) * hpg):
                r0 = h * qb
                alpha = al_ref[r0:r0 + qb, :]
                for c in range(ckv_t.shape[1] // LANE):
                    cs = slice(c * LANE, (c + 1) * LANE)
                    acc_ref[r0:r0 + qb, cs] = acc_ref[r0:r0 + qb, cs] * alpha + pv_ref[r0:r0 + qb, cs]
        return carry

    lax.fori_loop(0, nkt, attend, 0)
    for h in range(n_heads):
        r0 = h * qb
        lrow = jnp.sum(l_ref[r0:r0 + qb, :], axis=-1, keepdims=True)
        o_lat = (acc_ref[r0:r0 + qb, :] / lrow).astype(BF16)
        o_ref[:, h * dv:(h + 1) * dv] = _dot(o_lat, wuv_ref[h]).astype(o_ref.dtype)


def dsa_attention(qcat, proj_a, ckvn, w_uk, w_uv, batch, seq, *, n_heads, n_idx_heads, d_idx, dh, dv,
                  kidx_col_block, small_col_block):
    t_tok = batch * seq
    kv_lora = ckvn.shape[1]
    qb = min(Q_BLOCK, seq)
    nb = seq // qb
    tk = min(512, seq)
    topk = min(TOPK_MAX, seq // 4)
    qw = n_heads * dh
    qiw = n_idx_heads * d_idx
    assert qw == qiw and kv_lora <= tk
    kern = functools.partial(_dsa_kernel, n_heads=n_heads, n_idx_heads=n_idx_heads, d_idx=d_idx, dh=dh, dv=dv,
                             tk=tk, topk=topk)
    once = pl.Buffered(1)
    return pl.pallas_call(
        kern,
        out_shape=jax.ShapeDtypeStruct((t_tok, n_heads * dv), BF16),
        grid=(batch, nb),
        in_specs=[
            pl.BlockSpec((qb, qw), lambda b, i: (b * nb + i, 0)),
            pl.BlockSpec((qb, qiw), lambda b, i: (b * nb + i, 1)),
            pl.BlockSpec((qb, LANE), lambda b, i: (b * nb + i, small_col_block)),
            pl.BlockSpec((seq, d_idx), lambda b, i: (b, kidx_col_block), pipeline_mode=once),
            pl.BlockSpec((seq, kv_lora), lambda b, i: (b, 0), pipeline_mode=once),
            pl.BlockSpec(w_uk.shape, lambda b, i: (0, 0, 0), pipeline_mode=once),
            pl.BlockSpec(w_uv.shape, lambda b, i: (0, 0, 0), pipeline_mode=once),
        ],
        out_specs=pl.BlockSpec((qb, n_heads * dv), lambda b, i: (b * nb + i, 0)),
        scratch_shapes=[
            pltpu.VMEM((seq, qb), I32),
            pltpu.VMEM((qb, seq), F32),
            pltpu.VMEM((tk, n_idx_heads * qb), F32),
            pltpu.VMEM((n_heads * qb, tk), F32),
            pltpu.VMEM((n_heads * qb, tk), BF16),
            pltpu.VMEM((n_heads * qb, kv_lora), F32),
            pltpu.VMEM((n_heads * qb, kv_lora), F32),
            pltpu.VMEM((n_heads * qb, kv_lora), BF16),
            pltpu.VMEM((n_idx_heads * qb, d_idx), BF16),
            pltpu.VMEM((n_idx_heads, LANE, qb), F32),
            pltpu.VMEM((n_heads * qb, LANE), F32),
            pltpu.VMEM((n_heads * qb, LANE), F32),
            pltpu.VMEM((n_heads * qb, LANE), F32),
        ],
        compiler_params=_cparams(("arbitrary", "arbitrary")),
        name="dsa_attention",
    )(qcat, qcat, proj_a, proj_a, ckvn, w_uk, w_uv)


def _split_bf16(x):
    hi = x.astype(BF16)
    lo = (x - hi.astype(F32)).astype(BF16)
    return hi, lo


def _dot3(a, b):
    ah, al = _split_bf16(a)
    bh, bl = _split_bf16(b)
    lhs = jnp.concatenate([ah, ah, al], axis=1)
    rhs = jnp.concatenate([bh, bl, bh], axis=0)
    return _dot(lhs, rhs)


def _cumsum_rows(x):
    n = x.shape[0]
    row = lax.broadcasted_iota(I32, x.shape, 0)
    s = 1
    while s < n:
        x = x + jnp.where(row >= s, pltpu.roll(x, s, 0), 0.0)
        s *= 2
    return x


def _gdn_kernel(xq_ref, xk_ref, xv_ref, z_ref, gate_ref, cwq_ref, cwk_ref, cwv_ref, alog_ref, dtb_ref, gn_ref,
                o_ref, ext_ref, state_ref, *, pairs, dk, b_col, a_col, conv_width):
    c = CHUNK
    assert dk == 2 * c
    n = pl.program_id(2)
    hg = pl.program_id(1)
    width = pairs * 2 * dk
    halo = 8

    @pl.when(n == 0)
    def _():
        ext_ref[:, 0:halo, :] = jnp.zeros((3, halo, width), F32)
        state_ref[...] = jnp.zeros(state_ref.shape, F32)

    prs = range(pairs)

    def conv_silu(idx, cw_ref, p):
        halves = []
        for hh in range(2):
            lo = (2 * p + hh) * dk
            win = ext_ref[idx, :, lo:lo + dk]
            y = win[halo:, :] * cw_ref[conv_width - 1:conv_width, lo:lo + dk]
            for j in range(conv_width - 1):
                back = conv_width - 1 - j
                y = y + pltpu.roll(win, back, 0)[halo:, :] * cw_ref[j:j + 1, lo:lo + dk]
            halves.append(y * _sigmoid(y))
        return jnp.concatenate(halves, axis=0)

    for idx, x_ref in enumerate((xq_ref, xk_ref, xv_ref)):
        ext_ref[idx, halo:halo + c, :] = x_ref[...]
    q = [conv_silu(0, cwq_ref, p) for p in prs]
    k = [conv_silu(1, cwk_ref, p) for p in prs]
    v = [conv_silu(2, cwv_ref, p) for p in prs]
    for idx in range(3):
        ext_ref[idx, 0:halo, :] = ext_ref[idx, c:c + halo, :]

    gates = gate_ref[...]
    beta_all = _sigmoid(gates)
    sp = jnp.maximum(gates + dtb_ref[...], 0.0) + jnp.log(1.0 + jnp.exp(-jnp.abs(gates + dtb_ref[...])))
    g_all = -jnp.exp(alog_ref[...]) * sp
    gcum_all = _cumsum_rows(g_all)

    r2 = lax.broadcasted_iota(I32, (2 * c, 2 * c), 0)
    c2 = lax.broadcasted_iota(I32, (2 * c, 2 * c), 1)
    same = (r2 ^ c2) < c
    incl = same & (r2 >= c2)
    strict = same & (r2 > c2)
    top = lax.broadcasted_iota(I32, (2 * c, dk), 0) < c

    eye = jnp.where(r2 == c2, 1.0, 0.0)
    lane = lax.broadcasted_iota(I32, (c, LANE), 1)

    def pick(x, base, p):
        h0 = (hg * pairs + p) * 2
        a = jnp.sum(jnp.where(lane == base + h0, x, 0.0), axis=-1, keepdims=True)
        b = jnp.sum(jnp.where(lane == base + h0 + 1, x, 0.0), axis=-1, keepdims=True)
        return jnp.concatenate([jnp.broadcast_to(a, (c, dk)), jnp.broadcast_to(b, (c, dk))], axis=0)

    def bdot(a, b):
        return _dot(a.astype(BF16), b.astype(BF16))

    q = [x * lax.rsqrt(jnp.sum(x * x, axis=-1, keepdims=True) + EPS) * (dk ** -0.5) for x in q]
    k = [x * lax.rsqrt(jnp.sum(x * x, axis=-1, keepdims=True) + EPS) for x in k]
    beta = [pick(beta_all, b_col, p) for p in prs]
    gc = [pick(gcum_all, a_col, p) for p in prs]
    glast = [jnp.concatenate([jnp.broadcast_to(g[c - 1:c, :], (c, dk)),
                              jnp.broadcast_to(g[2 * c - 1:2 * c, :], (c, dk))], axis=0) for g in gc]
    eg = [jnp.exp(g) for g in gc]
    decay = [jnp.exp(jnp.where(incl, g - g.T, -jnp.inf)) for g in gc]
    kb = [k[p] * beta[p] for p in prs]
    kbf = [x.astype(BF16) for x in k]
    kk = [_dot_nt(kb[p].astype(BF16), kbf[p]) for p in prs]
    qk = [_dot_nt(q[p].astype(BF16), kbf[p]) for p in prs]
    nmat = [jnp.where(strict, -(kk[p] * decay[p]), 0.0) for p in prs]

    tinv = [eye + nm for nm in nmat]
    pw = nmat
    for _ in range(5):
        pw = [bdot(x, x) for x in pw]
        tinv = [tinv[p] + bdot(tinv[p], pw[p]) for p in prs]

    rhs = [jnp.concatenate([v[p] * beta[p], kb[p] * eg[p]], axis=1) for p in prs]
    x1 = [bdot(tinv[p], rhs[p]) for p in prs]
    resid = [rhs[p] - x1[p] + _dot3(nmat[p], x1[p]) for p in prs]
    xs = [x1[p] + bdot(tinv[p], resid[p]) for p in prs]
    u = [x[:, 0:dk] for x in xs]
    w = [x[:, dk:2 * dk] for x in xs]

    amat = [jnp.where(incl, qk[p] * decay[p], 0.0) for p in prs]
    q_dec = [q[p] * eg[p] for p in prs]
    k_dec_t = [(k[p] * jnp.exp(glast[p] - gc[p])).T.astype(BF16) for p in prs]
    cd = [jnp.exp(g) for g in glast]

    s2 = [state_ref[p] for p in prs]
    s2b = [x.astype(BF16) for x in s2]
    ws = [_dot(w[p].astype(BF16), s2b[p]) for p in prs]
    qs = [_dot(q_dec[p].astype(BF16), s2b[p]) for p in prs]
    v_new = [u[p] - jnp.where(top, ws[p][:, 0:dk], ws[p][:, dk:2 * dk]) for p in prs]
    vb = [x.astype(BF16) for x in v_new]
    o = [jnp.where(top, qs[p][:, 0:dk], qs[p][:, dk:2 * dk]) + _dot(amat[p].astype(BF16), vb[p]) for p in prs]
    zero = jnp.zeros((2 * c, dk), BF16)
    vexp = [jnp.concatenate([jnp.where(top, x, zero), jnp.where(top, zero, x)], axis=1) for x in vb]
    upd = [_dot(k_dec_t[p], vexp[p]) for p in prs]
    for p in prs:
        cdrow = jnp.concatenate([cd[p][0:1, :], cd[p][c:c + 1, :]], axis=1)
        state_ref[p] = s2[p] * cdrow + upd[p]

    for p in prs:
        on = o[p] * lax.rsqrt(jnp.mean(o[p] * o[p], axis=-1, keepdims=True) + EPS) * gn_ref[...]
        for hh in range(2):
            lo = (2 * p + hh) * dk
            zz = z_ref[:, lo:lo + dk]
            y = on[hh * c:(hh + 1) * c, :] * (zz * _sigmoid(zz))
            o_ref[:, lo:lo + dk] = y.astype(o_ref.dtype)


def gated_deltanet(proj_b, proj_small, conv_w, a_log, dt_bias, g_out, batch, seq, *, n_heads, dk,
                   small_col_block, small_row_array_cols, b_col, a_col, pairs):
    t_tok = batch * seq
    nchunk = seq // CHUNK
    width = pairs * 2 * dk
    ng = n_heads // (2 * pairs)
    hw = n_heads * dk
    sec = hw // width
    conv_width = conv_w.shape[0]
    alog_row = jnp.zeros((1, LANE), F32).at[0, a_col:a_col + n_heads].set(a_log.astype(F32))
    dtb_row = jnp.zeros((1, LANE), F32).at[0, a_col:a_col + n_heads].set(dt_bias.astype(F32))
    kern = functools.partial(_gdn_kernel, pairs=pairs, dk=dk, b_col=b_col, a_col=a_col, conv_width=conv_width)
    row = lambda b, g, n: b * nchunk + n
    return pl.pallas_call(
        kern,
        out_shape=jax.ShapeDtypeStruct((t_tok, hw), BF16),
        grid=(batch, ng, nchunk),
        in_specs=[
            pl.BlockSpec((CHUNK, width), lambda b, g, n: (row(b, g, n), g)),
            pl.BlockSpec((CHUNK, width), lambda b, g, n: (row(b, g, n), sec + g)),
            pl.BlockSpec((CHUNK, width), lambda b, g, n: (row(b, g, n), 2 * sec + g)),
            pl.BlockSpec((CHUNK, width), lambda b, g, n: (row(b, g, n), 3 * sec + g)),
            pl.BlockSpec((CHUNK, LANE), lambda b, g, n: (row(b, g, n), small_col_block)),
            pl.BlockSpec((conv_width, width), lambda b, g, n: (0, g)),
            pl.BlockSpec((conv_width, width), lambda b, g, n: (0, sec + g)),
            pl.BlockSpec((conv_width, width), lambda b, g, n: (0, 2 * sec + g)),
            pl.BlockSpec((1, LANE), lambda b, g, n: (0, 0)),
            pl.BlockSpec((1, LANE), lambda b, g, n: (0, 0)),
            pl.BlockSpec((1, dk), lambda b, g, n: (0, 0)),
        ],
        out_specs=pl.BlockSpec((CHUNK, width), lambda b, g, n: (row(b, g, n), g)),
        scratch_shapes=[pltpu.VMEM((3, CHUNK + 8, width), F32),
                        pltpu.VMEM((pairs, dk, 2 * dk), F32)],
        compiler_params=_cparams(("arbitrary", "arbitrary", "arbitrary")),
        name="gated_deltanet",
    )(proj_b, proj_b, proj_b, proj_b, proj_small, conv_w, conv_w, conv_w, alog_row, dtb_row,
      g_out.reshape(1, dk).astype(F32))


def _cross_kernel(h_ref, gq_ref, wcq_ref, k_ref, v_ref, wco_ref, g_ref, h_out_ref, n_out_ref, *, n_heads, dh):
    scale = dh ** -0.5
    h_in = h_ref[...]
    ms = jnp.mean(h_in * h_in, axis=-1, keepdims=True)
    hq = (h_in * lax.rsqrt(ms + EPS) * gq_ref[...]).astype(BF16)
    q = _dot(hq, wcq_ref[...]).astype(BF16)
    outs = []
    for h in range(n_heads):
        s = _dot_nt(q[:, h * dh:(h + 1) * dh], k_ref[:, h * dh:(h + 1) * dh]) * scale
        m = jnp.max(s, axis=-1, keepdims=True)
        p = jnp.exp(s - m)
        l = jnp.sum(p, axis=-1, keepdims=True)
        p = (p / l).astype(BF16)
        outs.append(_dot(p, v_ref[:, h * dh:(h + 1) * dh]).astype(BF16))
    o = jnp.concatenate(outs, axis=1)
    hn = h_in + _dot(o, wco_ref[...])
    h_out_ref[...] = hn
    ms = jnp.mean(hn * hn, axis=-1, keepdims=True)
    n_out_ref[...] = (hn * lax.rsqrt(ms + EPS) * g_ref[...]).astype(n_out_ref.dtype)


def cross_attention(h, g_q, w_cq, kvx, w_co, g_next, batch, seq, n_mem, *, n_heads, dh, tq=256):
    t_tok, d = h.shape
    tq = min(tq, seq)
    nq = seq // tq
    hw = n_heads * dh
    kern = functools.partial(_cross_kernel, n_heads=n_heads, dh=dh)
    once = pl.Buffered(1)
    return pl.pallas_call(
        kern,
        out_shape=(jax.ShapeDtypeStruct((t_tok, d), F32), jax.ShapeDtypeStruct((t_tok, d), BF16)),
        grid=(batch, nq),
        in_specs=[
            pl.BlockSpec((tq, d), lambda b, i: (b * nq + i, 0)),
            pl.BlockSpec((1, d), lambda b, i: (0, 0)),
            pl.BlockSpec((d, hw), lambda b, i: (0, 0), pipeline_mode=once),
            pl.BlockSpec((n_mem, hw), lambda b, i: (b, 0)),
            pl.BlockSpec((n_mem, hw), lambda b, i: (b, 1)),
            pl.BlockSpec((hw, d), lambda b, i: (0, 0), pipeline_mode=once),
            pl.BlockSpec((1, d), lambda b, i: (0, 0)),
        ],
        out_specs=(pl.BlockSpec((tq, d), lambda b, i: (b * nq + i, 0)),
                   pl.BlockSpec((tq, d), lambda b, i: (b * nq + i, 0))),
        compiler_params=_cparams(("parallel", "parallel")),
        name="cross_attention",
    )(h, g_q.reshape(1, d).astype(F32), w_cq, kvx, kvx, w_co, g_next.reshape(1, d).astype(F32))


def _pad_cols(w, n):
    return jnp.pad(w, ((0, 0), (0, n - w.shape[1])))


def kernel(x, mem, attn_norm_g, w_in, qa_norm_g, w_qb, kv_norm_g, w_uk, w_uv, w_iq, conv_w, a_log, dt_bias,
           delta_norm_g, w_o, cross_norm_g, mem_norm_g, w_cq, w_ckv, w_co, ffn_norm_g, w_ffn_in, w_ffn_out,
           final_norm_g):
    batch, seq, d = x.shape
    n_mem = mem.shape[1]
    depth = w_in.shape[0]
    q_lora = qa_norm_g.shape[1]
    kv_lora = kv_norm_g.shape[1]
    h_a, _, dh_a = w_uk.shape[1:]
    dv_a = w_uv.shape[3]
    h_idx = w_iq.shape[2] // 128
    d_idx = w_iq.shape[2] // h_idx
    h_b = a_log.shape[1]
    dk_b = delta_norm_g.shape[1]
    h_x = w_cq.shape[2] // 128
    dh_x = w_cq.shape[2] // h_x
    d_ff = w_ffn_out.shape[1]
    t_tok = batch * seq
    hw_b = h_b * dk_b

    h = x.reshape(t_tok, d)
    memf = mem.reshape(batch * n_mem, d)
    for l in range(depth):
        win = w_in[l]
        o_qa, o_ckv, o_kidx, o_widx = 0, q_lora, q_lora + kv_lora, q_lora + kv_lora + d_idx
        o_qkv = o_widx + h_idx
        o_z = o_qkv + 3 * hw_b
        o_b = o_z + hw_b
        o_a = o_b + h_b
        small = jnp.concatenate([win[:, o_widx:o_widx + h_idx], win[:, o_b:o_b + h_b], win[:, o_a:o_a + h_b]], axis=1)
        w_a = jnp.concatenate([win[:, o_qa:o_widx], _pad_cols(small, LANE)], axis=1).astype(BF16)
        w_b = win[:, o_qkv:o_qkv + 4 * hw_b]
        na = w_a.shape[1]
        assert q_lora % kv_lora == 0 and (q_lora + kv_lora) % d_idx == 0 and d_idx == LANE
        kidx_cb = (q_lora + kv_lora) // d_idx
        small_cb = kidx_cb + 1
        w_q2 = jnp.concatenate([w_qb[l], w_iq[l]], axis=1).astype(BF16)
        w_uk_b = w_uk[l].astype(BF16)
        w_uv_b = w_uv[l].astype(BF16)
        w_o_b = w_o[l]
        w_cq_b = w_cq[l].astype(BF16)
        w_ckv_b = w_ckv[l].astype(BF16)
        w_co_b = w_co[l].astype(BF16)
        ffp = -(-d_ff // 1024) * 1024
        w_fi_b = w_ffn_in[l]
        w_fo_b = jnp.pad(w_ffn_out[l], ((0, ffp - d_ff), (0, 0))).astype(BF16)

        n0 = rmsnorm_cols(h, attn_norm_g[l], d, 0, BF16)
        tn_a = na // 3 if na % (3 * LANE) == 0 else na
        proj_a = matmul([(n0, 0, w_a, 0, d)], na, F32, tm=1024, tn=tn_a, name="in_proj_a")
        proj_b = matmul([(n0, 0, w_b, 0, d)], 4 * hw_b, F32, tm=1024, tn=512, name="in_proj_b")
        qan = rmsnorm_cols(proj_a, qa_norm_g[l], q_lora, 0, BF16)
        ckvn = rmsnorm_cols(proj_a, kv_norm_g[l], kv_lora, q_lora // kv_lora, BF16)
        qcat = matmul([(qan, 0, w_q2, 0, q_lora)], w_q2.shape[1], BF16, tm=1024, tn=1024, name="q_proj")
        y_a = dsa_attention(qcat, proj_a, ckvn, w_uk_b, w_uv_b, batch, seq, n_heads=h_a, n_idx_heads=h_idx,
                            d_idx=d_idx, dh=dh_a, dv=dv_a, kidx_col_block=kidx_cb, small_col_block=small_cb)
        y_b = gated_deltanet(proj_b, proj_a, conv_w[l], a_log[l], dt_bias[l], delta_norm_g[l], batch, seq,
                             n_heads=h_b, dk=dk_b, small_col_block=small_cb, small_row_array_cols=na,
                             b_col=h_idx, a_col=h_idx + h_b, pairs=min(8, h_b // 2))
        k_a = h_a * dv_a
        assert k_a == hw_b
        h = matmul([(y_a, 0, w_o_b, 0, k_a), (y_b, 0, w_o_b, 1, k_a)], d, F32, tm=1024, tn=512, residual=h,
                   name="out_proj")

        memn = rmsnorm_cols(memf, mem_norm_g[l], d, 0, BF16)
        kvx = matmul([(memn, 0, w_ckv_b, 0, d)], w_ckv_b.shape[1], BF16, tm=1024, tn=512, name="mem_kv_proj")
        h, n2 = cross_attention(h, cross_norm_g[l], w_cq_b, kvx, w_co_b, ffn_norm_g[l], batch, seq, n_mem,
                                n_heads=h_x, dh=dh_x)

        act = matmul_swiglu(n2, w_fi_b, d_ff, ffp, tm=2048, tn=256)
        h = matmul([(act, 0, w_fo_b, 0, ffp)], d, F32, tm=1024, tn=1024, tk=ffp // 4, residual=h, name="ffn_out")
    out = rmsnorm_cols(h, final_norm_g, d, 0, F32)
    return out.reshape(batch, seq, d)
```

```python
import functools

import jax
import jax.numpy as jnp
from jax import lax
from jax.experimental import pallas as pl
from jax.experimental.pallas import tpu as pltpu

F32 = jnp.float32
BF16 = jnp.bfloat16
I32 = jnp.int32

EPS = 1e-6
CHUNK = 64
Q_BLOCK = 128
TOPK_MAX = 256
LANE = 128
VMEM_LIMIT = 56 * 1024 * 1024
INT_MIN = -2 ** 31
CHUNK_SHIFT = CHUNK.bit_length() - 1
assert 1 << CHUNK_SHIFT == CHUNK
NEG_BIG = -1e30


def _cparams(sem):
    return pltpu.CompilerParams(dimension_semantics=sem, vmem_limit_bytes=VMEM_LIMIT)


def _dot(a, b):
    return jnp.dot(a, b, preferred_element_type=F32)


def _dot_nt(a, b):
    return lax.dot_general(a, b, (((1,), (1,)), ((), ())), preferred_element_type=F32)


def _sigmoid(x):
    return 1.0 / (1.0 + jnp.exp(-x))


def _rmsnorm_kernel(x_ref, g_ref, o_ref):
    x = x_ref[...].astype(F32)
    ms = jnp.mean(x * x, axis=-1, keepdims=True)
    o_ref[...] = (x * lax.rsqrt(ms + EPS) * g_ref[...]).astype(o_ref.dtype)


def rmsnorm_cols(x, g, width, col_block, out_dtype, tr=256):
    m = x.shape[0]
    tr = min(tr, m)
    return pl.pallas_call(
        _rmsnorm_kernel,
        out_shape=jax.ShapeDtypeStruct((m, width), out_dtype),
        grid=(m // tr,),
        in_specs=[pl.BlockSpec((tr, width), lambda i: (i, col_block)),
                  pl.BlockSpec((1, width), lambda i: (0, 0))],
        out_specs=pl.BlockSpec((tr, width), lambda i: (i, 0)),
        compiler_params=_cparams(("parallel",)),
        name="rmsnorm",
    )(x, g.reshape(1, width).astype(F32))


def _mm_kernel(*refs, n_pairs, nk, has_res, b_is_nk):
    a_refs = refs[0:2 * n_pairs:2]
    b_refs = refs[1:2 * n_pairs:2]
    pos = 2 * n_pairs
    r_ref = refs[pos] if has_res else None
    pos += int(has_res)
    o_ref = refs[pos]
    acc_ref = refs[pos + 1] if nk > 1 else None

    dot = _dot_nt if b_is_nk else _dot
    part = dot(a_refs[0][...], b_refs[0][...].astype(BF16))
    for a_ref, b_ref in zip(a_refs[1:], b_refs[1:]):
        part = part + dot(a_ref[...], b_ref[...].astype(BF16))

    def finish(acc):
        if has_res:
            acc = acc + r_ref[...]
        o_ref[...] = acc.astype(o_ref.dtype)

    if nk == 1:
        finish(part)
    else:
        k = pl.program_id(2)

        @pl.when(k == 0)
        def _():
            acc_ref[...] = part

        @pl.when(k > 0)
        def _():
            acc_ref[...] += part

        @pl.when(k == nk - 1)
        def _():
            finish(acc_ref[...])


def matmul(pairs, n, out_dtype, tm, tn, tk=None, residual=None, name="matmul", b_nk_row0=None):
    m = pairs[0][0].shape[0]
    tm = min(tm, m)
    tn = min(tn, n)
    kdim = pairs[0][4]
    tk = kdim if tk is None else min(tk, kdim)
    nk = kdim // tk
    assert m % tm == 0 and n % tn == 0 and kdim % tk == 0
    in_specs, args = [], []
    for a, acb, b, brb, kd in pairs:
        assert kd == kdim
        in_specs.append(pl.BlockSpec((tm, tk), functools.partial(lambda i, j, k, o: (i, o + k), o=acb * nk)))
        if b_nk_row0 is None:
            in_specs.append(pl.BlockSpec((tk, tn), functools.partial(lambda i, j, k, o: (o + k, j), o=brb * nk)))
        else:
            in_specs.append(pl.BlockSpec(
                (pl.Element(tn), pl.Element(tk)),
                functools.partial(lambda i, j, k, o: ((b_nk_row0 // 8 + j * (tn // 8)) * 8, (o + k) * tk),
                                  o=brb * nk)))
        args += [a, b]
    if residual is not None:
        in_specs.append(pl.BlockSpec((tm, tn), lambda i, j, k: (i, j)))
        args.append(residual)
    return pl.pallas_call(
        functools.partial(_mm_kernel, n_pairs=len(pairs), nk=nk, has_res=residual is not None,
                          b_is_nk=b_nk_row0 is not None),
        out_shape=jax.ShapeDtypeStruct((m, n), out_dtype),
        grid=(m // tm, n // tn, nk),
        in_specs=in_specs,
        out_specs=pl.BlockSpec((tm, tn), lambda i, j, k: (i, j)),
        scratch_shapes=[pltpu.VMEM((tm, tn), F32)] if nk > 1 else [],
        compiler_params=_cparams(("parallel", "parallel", "arbitrary")),
        name=name,
    )(*args)


def _swiglu_kernel(a_ref, bg_ref, bu_ref, o_ref, *, nj):
    j = pl.program_id(1)

    @pl.when(j < nj)
    def _():
        a = a_ref[...]
        gate = _dot(a, bg_ref[...].astype(BF16))
        up = _dot(a, bu_ref[...].astype(BF16))
        o_ref[...] = (gate * _sigmoid(gate) * up).astype(o_ref.dtype)

    @pl.when(j >= nj)
    def _():
        o_ref[...] = jnp.zeros(o_ref.shape, o_ref.dtype)


def matmul_swiglu(a, w, d_ff, d_out, tm, tn):
    m, k = a.shape
    tm = min(tm, m)
    tn = min(tn, d_ff)
    assert d_ff % tn == 0 and d_out % tn == 0
    nj = d_ff // tn
    return pl.pallas_call(
        functools.partial(_swiglu_kernel, nj=nj),
        out_shape=jax.ShapeDtypeStruct((m, d_out), BF16),
        grid=(m // tm, d_out // tn),
        in_specs=[pl.BlockSpec((tm, k), lambda i, j: (i, 0), pipeline_mode=pl.Buffered(1)),
                  pl.BlockSpec((k, tn), lambda i, j: (0, jnp.minimum(j, nj - 1))),
                  pl.BlockSpec((k, tn), lambda i, j: (0, jnp.minimum(j, nj - 1) + nj))],
        out_specs=pl.BlockSpec((tm, tn), lambda i, j: (i, j)),
        compiler_params=_cparams(("parallel", "parallel")),
        name="ffn_in_swiglu",
    )(a, w, w)


def _dsa_kernel(q_ref, qi_ref, sm_ref, kidx_ref, ckv_ref, wuk_ref, wuv_ref, o_ref,
                key_ref, bias_ref, st_ref, s_ref, p_ref, pv_ref, acc_ref, ql_ref, qis_ref, wb_ref, m_ref, l_ref, al_ref,
                *, n_heads, n_idx_heads, d_idx, dh, dv, tk, topk):
    qb = q_ref.shape[0]
    i = pl.program_id(1)
    n_keys = (i + 1) * qb
    nkt = (n_keys + tk - 1) // tk
    idx_scale = d_idx ** -0.5 * n_idx_heads ** -0.5
    att_scale = dh ** -0.5
    nlc = tk // LANE

    assert qb == LANE
    w_t = sm_ref[...].T * idx_scale
    for h in range(n_idx_heads):
        wb_ref[h] = jnp.broadcast_to(w_t[h:h + 1, :], (LANE, qb))
        qis_ref[h * qb:(h + 1) * qb, :] = qi_ref[:, h * d_idx:(h + 1) * d_idx]

    q_lane = lax.broadcasted_iota(I32, (LANE, qb), 1)
    k_row = lax.broadcasted_iota(I32, (LANE, qb), 0)
    lim = (((i * qb + q_lane) >> CHUNK_SHIFT) + 1) << CHUNK_SHIFT

    def score_tile(t, carry):
        ks = pl.multiple_of(t * tk, tk)
        kblk = kidx_ref[pl.ds(ks, tk), :].astype(BF16)
        st_ref[...] = _dot_nt(kblk, qis_ref[...])
        for c in range(nlc):
            sc = jnp.zeros((LANE, qb), F32)
            for h in range(n_idx_heads):
                sc = sc + jnp.maximum(st_ref[c * LANE:(c + 1) * LANE, h * qb:(h + 1) * qb], 0.0) * wb_ref[h]
            bits = pltpu.bitcast(sc, I32)
            key = bits ^ ((bits >> 31) & 0x7FFFFFFF)
            key = jnp.where(ks + c * LANE + k_row < lim, key, INT_MIN)
            key_ref[pl.ds(ks + c * LANE, LANE), :] = key
        return carry

    lax.fori_loop(0, nkt, score_tile, 0)

    pos_tile = lax.broadcasted_iota(I32, (tk, qb), 0)
    acc_rows = min(64, tk)

    def count(pred):
        def body(t, c):
            ks = pl.multiple_of(t * tk, tk)
            hit = jnp.where(pred(key_ref[pl.ds(ks, tk), :], ks + pos_tile), 1.0, 0.0)
            return c + jnp.sum(hit.reshape(tk // acc_rows, acc_rows, qb), axis=0)
        c = lax.fori_loop(0, nkt, body, jnp.zeros((acc_rows, qb), F32))
        return jnp.sum(c, axis=0, keepdims=True)

    kf = float(topk)
    t0 = jnp.where(count(lambda k, p: k >= 0) >= kf, 0, INT_MIN).astype(I32)

    def bisect(j, t):
        cand = t + (jnp.int32(1) << (30 - j))
        return jnp.where(count(lambda k, p: k >= cand) >= kf, cand, t)

    thr = lax.fori_loop(0, 31, bisect, t0)
    thr = jnp.maximum(thr, INT_MIN + 1)

    n_ge = count(lambda k, p: k >= thr)

    def tie_search():
        need = kf - count(lambda k, p: k > thr)

        def step(j, lo):
            cand = lo + (jnp.int32(1) << (pos_bits - 1 - j))
            below = count(lambda k, p: (k == thr) & (p < cand))
            return jnp.where(below < need, cand, lo)

        return lax.fori_loop(0, pos_bits, step, jnp.zeros((1, qb), I32))

    pos_bits = max(1, (key_ref.shape[0] - 1).bit_length())
    last = lax.cond(jnp.max(n_ge) > kf, tie_search, lambda: jnp.full((1, qb), key_ref.shape[0], I32))

    def bias_tile(t, carry):
        ks = pl.multiple_of(t * tk, tk)
        for c in range(nlc):
            k = key_ref[pl.ds(ks + c * LANE, LANE), :]
            tied = jnp.where(ks + c * LANE + k_row <= last, 0.0, NEG_BIG)
            b = jnp.where(k > thr, 0.0, jnp.where(k == thr, tied, NEG_BIG))
            bias_ref[:, pl.ds(ks + c * LANE, LANE)] = b.T
        return carry

    lax.fori_loop(0, nkt, bias_tile, 0)

    rows = n_heads * qb
    for h in range(n_heads):
        ql = _dot_nt(q_ref[:, h * dh:(h + 1) * dh], wuk_ref[h]) * att_scale
        ql_ref[h * qb:(h + 1) * qb, :] = ql.astype(BF16)
    m_ref[...] = jnp.full((rows, LANE), NEG_BIG, F32)
    l_ref[...] = jnp.zeros((rows, LANE), F32)
    acc_ref[...] = jnp.zeros(acc_ref.shape, F32)

    hpg = 4 if n_heads % 4 == 0 else 1
    grows = hpg * qb

    def attend(t, carry):
        ks = pl.multiple_of(t * tk, tk)
        ckv_t = ckv_ref[pl.ds(ks, tk), :]
        for g in range(n_heads // hpg):
            g0 = g * grows
            s_ref[g0:g0 + grows, :] = _dot_nt(ql_ref[g0:g0 + grows, :], ckv_t)
        for g in range(n_heads // hpg):
            g0 = g * grows
            for h in range(g * hpg, (g + 1) * hpg):
                r0 = h * qb

                def masked(c):
                    return s_ref[r0:r0 + qb, c * LANE:(c + 1) * LANE] + bias_ref[:, pl.ds(ks + c * LANE, LANE)]

                tmax = masked(0)
                for c in range(1, nlc):
                    tmax = jnp.maximum(tmax, masked(c))
                m_old = m_ref[r0:r0 + qb, :]
                m_new = jnp.maximum(m_old, jnp.max(tmax, axis=-1, keepdims=True))
                alpha = jnp.exp(m_old - m_new)
                lsum = l_ref[r0:r0 + qb, :] * alpha
                for c in range(nlc):
                    p = jnp.exp(masked(c) - m_new)
                    lsum = lsum + p
                    p_ref[r0:r0 + qb, c * LANE:(c + 1) * LANE] = p.astype(BF16)
                m_ref[r0:r0 + qb, :] = m_new
                l_ref[r0:r0 + qb, :] = lsum
                al_ref[r0:r0 + qb, :] = alpha
            pv_ref[g0:g0 + grows, :] = _dot(p_ref[g0:g0 + grows, :], ckv_t)
            for h in range(g * hpg, (g + 1) * hpg):
                r0 = h * qb
                alpha = al_ref[r0:r0 + qb, :]
                for c in range(ckv_t.shape[1] // LANE):
                    cs = slice(c * LANE, (c + 1) * LANE)
                    acc_ref[r0:r0 + qb, cs] = acc_ref[r0:r0 + qb, cs] * alpha + pv_ref[r0:r0 + qb, cs]
        return carry

    lax.fori_loop(0, nkt, attend, 0)
    for h in range(n_heads):
        r0 = h * qb
        lrow = jnp.sum(l_ref[r0:r0 + qb, :], axis=-1, keepdims=True)
        o_lat = (acc_ref[r0:r0 + qb, :] / lrow).astype(BF16)
        o_ref[:, h * dv:(h + 1) * dv] = _dot(o_lat, wuv_ref[h]).astype(o_ref.dtype)


def dsa_attention(qcat, proj_a, ckvn, w_uk, w_uv, batch, seq, *, n_heads, n_idx_heads, d_idx, dh, dv,
                  kidx_col_block, small_col_block):
    t_tok = batch * seq
    kv_lora = ckvn.shape[1]
    qb = min(Q_BLOCK, seq)
    nb = seq // qb
    tk = min(512, seq)
    topk = min(TOPK_MAX, seq // 4)
    qw = n_heads * dh
    qiw = n_idx_heads * d_idx
    assert qw == qiw and kv_lora <= tk
    kern = functools.partial(_dsa_kernel, n_heads=n_heads, n_idx_heads=n_idx_heads, d_idx=d_idx, dh=dh, dv=dv,
                             tk=tk, topk=topk)
    once = pl.Buffered(1)
    return pl.pallas_call(
        kern,
        out_shape=jax.ShapeDtypeStruct((t_tok, n_heads * dv), BF16),
        grid=(batch, nb),
        in_specs=[
            pl.BlockSpec((qb, qw), lambda b, i: (b * nb + i, 0)),
            pl.BlockSpec((qb, qiw), lambda b, i: (b * nb + i, 1)),
            pl.BlockSpec((qb, LANE), lambda b, i: (b * nb + i, small_col_block)),
            pl.BlockSpec((seq, d_idx), lambda b, i: (b, kidx_col_block), pipeline_mode=once),
            pl.BlockSpec((seq, kv_lora), lambda b, i: (b, 0), pipeline_mode=once),
            pl.BlockSpec(w_uk.shape, lambda b, i: (0, 0, 0), pipeline_mode=once),
            pl.BlockSpec(w_uv.shape, lambda b, i: (0, 0, 0), pipeline_mode=once),
        ],
        out_specs=pl.BlockSpec((qb, n_heads * dv), lambda b, i: (b * nb + i, 0)),
        scratch_shapes=[
            pltpu.VMEM((seq, qb), I32),
            pltpu.VMEM((qb, seq), F32),
            pltpu.VMEM((tk, n_idx_heads * qb), F32),
            pltpu.VMEM((n_heads * qb, tk), F32),
            pltpu.VMEM((n_heads * qb, tk), BF16),
            pltpu.VMEM((n_heads * qb, kv_lora), F32),
            pltpu.VMEM((n_heads * qb, kv_lora), F32),
            pltpu.VMEM((n_heads * qb, kv_lora), BF16),
            pltpu.VMEM((n_idx_heads * qb, d_idx), BF16),
            pltpu.VMEM((n_idx_heads, LANE, qb), F32),
            pltpu.VMEM((n_heads * qb, LANE), F32),
            pltpu.VMEM((n_heads * qb, LANE), F32),
            pltpu.VMEM((n_heads * qb, LANE), F32),
        ],
        compiler_params=_cparams(("arbitrary", "arbitrary")),
        name="dsa_attention",
    )(qcat, qcat, proj_a, proj_a, ckvn, w_uk, w_uv)


def _split_bf16(x):
    hi = x.astype(BF16)
    lo = (x - hi.astype(F32)).astype(BF16)
    return hi, lo


def _dot3(a, b):
    ah, al = _split_bf16(a)
    bh, bl = _split_bf16(b)
    lhs = jnp.concatenate([ah, ah, al], axis=1)
    rhs = jnp.concatenate([bh, bl, bh], axis=0)
    return _dot(lhs, rhs)


def _cumsum_rows(x):
    n = x.shape[0]
    row = lax.broadcasted_iota(I32, x.shape, 0)
    s = 1
    while s < n:
        x = x + jnp.where(row >= s, pltpu.roll(x, s, 0), 0.0)
        s *= 2
    return x


def _gdn_kernel(xq_ref, xk_ref, xv_ref, z_ref, gate_ref, cwq_ref, cwk_ref, cwv_ref, alog_ref, dtb_ref, gn_ref,
                o_ref, ext_ref, state_ref, *, pairs, dk, b_col, a_col, conv_width):
    c = CHUNK
    assert dk == 2 * c
    n = pl.program_id(2)
    hg = pl.program_id(1)
    width = pairs * 2 * dk
    halo = 8

    @pl.when(n == 0)
    def _():
        ext_ref[:, 0:halo, :] = jnp.zeros((3, halo, width), F32)
        state_ref[...] = jnp.zeros(state_ref.shape, F32)

    prs = range(pairs)

    def conv_silu(idx, cw_ref, p):
        halves = []
        for hh in range(2):
            lo = (2 * p + hh) * dk
            win = ext_ref[idx, :, lo:lo + dk]
            y = win[halo:, :] * cw_ref[conv_width - 1:conv_width, lo:lo + dk]
            for j in range(conv_width - 1):
                back = conv_width - 1 - j
                y = y + pltpu.roll(win, back, 0)[halo:, :] * cw_ref[j:j + 1, lo:lo + dk]
            halves.append(y * _sigmoid(y))
        return jnp.concatenate(halves, axis=0)

    for idx, x_ref in enumerate((xq_ref, xk_ref, xv_ref)):
        ext_ref[idx, halo:halo + c, :] = x_ref[...]
    q = [conv_silu(0, cwq_ref, p) for p in prs]
    k = [conv_silu(1, cwk_ref, p) for p in prs]
    v = [conv_silu(2, cwv_ref, p) for p in prs]
    for idx in range(3):
        ext_ref[idx, 0:halo, :] = ext_ref[idx, c:c + halo, :]

    gates = gate_ref[...]
    beta_all = _sigmoid(gates)
    sp = jnp.maximum(gates + dtb_ref[...], 0.0) + jnp.log(1.0 + jnp.exp(-jnp.abs(gates + dtb_ref[...])))
    g_all = -jnp.exp(alog_ref[...]) * sp
    gcum_all = _cumsum_rows(g_all)

    r2 = lax.broadcasted_iota(I32, (2 * c, 2 * c), 0)
    c2 = lax.broadcasted_iota(I32, (2 * c, 2 * c), 1)
    same = (r2 ^ c2) < c
    incl = same & (r2 >= c2)
    strict = same & (r2 > c2)
    top = lax.broadcasted_iota(I32, (2 * c, dk), 0) < c

    eye = jnp.where(r2 == c2, 1.0, 0.0)
    lane = lax.broadcasted_iota(I32, (c, LANE), 1)

    def pick(x, base, p):
        h0 = (hg * pairs + p) * 2
        a = jnp.sum(jnp.where(lane == base + h0, x, 0.0), axis=-1, keepdims=True)
        b = jnp.sum(jnp.where(lane == base + h0 + 1, x, 0.0), axis=-1, keepdims=True)
        return jnp.concatenate([jnp.broadcast_to(a, (c, dk)), jnp.broadcast_to(b, (c, dk))], axis=0)

    def bdot(a, b):
        return _dot(a.astype(BF16), b.astype(BF16))

    q = [x * lax.rsqrt(jnp.sum(x * x, axis=-1, keepdims=True) + EPS) * (dk ** -0.5) for x in q]
    k = [x * lax.rsqrt(jnp.sum(x * x, axis=-1, keepdims=True) + EPS) for x in k]
    beta = [pick(beta_all, b_col, p) for p in prs]
    gc = [pick(gcum_all, a_col, p) for p in prs]
    glast = [jnp.concatenate([jnp.broadcast_to(g[c - 1:c, :], (c, dk)),
                              jnp.broadcast_to(g[2 * c - 1:2 * c, :], (c, dk))], axis=0) for g in gc]
    eg = [jnp.exp(g) for g in gc]
    decay = [jnp.exp(jnp.where(incl, g - g.T, -jnp.inf)) for g in gc]
    kb = [k[p] * beta[p] for p in prs]
    kbf = [x.astype(BF16) for x in k]
    kk = [_dot_nt(kb[p].astype(BF16), kbf[p]) for p in prs]
    qk = [_dot_nt(q[p].astype(BF16), kbf[p]) for p in prs]
    nmat = [jnp.where(strict, -(kk[p] * decay[p]), 0.0) for p in prs]

    tinv = [eye + nm for nm in nmat]
    pw = nmat
    for _ in range(5):
        pw = [bdot(x, x) for x in pw]
        tinv = [tinv[p] + bdot(tinv[p], pw[p]) for p in prs]

    rhs = [jnp.concatenate([v[p] * beta[p], kb[p] * eg[p]], axis=1) for p in prs]
    x1 = [bdot(tinv[p], rhs[p]) for p in prs]
    resid = [rhs[p] - x1[p] + _dot3(nmat[p], x1[p]) for p in prs]
    xs = [x1[p] + bdot(tinv[p], resid[p]) for p in prs]
    u = [x[:, 0:dk] for x in xs]
    w = [x[:, dk:2 * dk] for x in xs]

    amat = [jnp.where(incl, qk[p] * decay[p], 0.0) for p in prs]
    q_dec = [q[p] * eg[p] for p in prs]
    k_dec_t = [(k[p] * jnp.exp(glast[p] - gc[p])).T.astype(BF16) for p in prs]
    cd = [jnp.exp(g) for g in glast]

    s2 = [state_ref[p] for p in prs]
    s2b = [x.astype(BF16) for x in s2]
    ws = [_dot(w[p].astype(BF16), s2b[p]) for p in prs]
    qs = [_dot(q_dec[p].astype(BF16), s2b[p]) for p in prs]
    v_new = [u[p] - jnp.where(top, ws[p][:, 0:dk], ws[p][:, dk:2 * dk]) for p in prs]
    vb = [x.astype(BF16) for x in v_new]
    o = [jnp.where(top, qs[p][:, 0:dk], qs[p][:, dk:2 * dk]) + _dot(amat[p].astype(BF16), vb[p]) for p in prs]
    zero = jnp.zeros((2 * c, dk), BF16)
    vexp = [jnp.concatenate([jnp.where(top, x, zero), jnp.where(top, zero, x)], axis=1) for x in vb]
    upd = [_dot(k_dec_t[p], vexp[p]) for p in prs]
    for p in prs:
        cdrow = jnp.concatenate([cd[p][0:1, :], cd[p][c:c + 1, :]], axis=1)
        state_ref[p] = s2[p] * cdrow + upd[p]

    for p in prs:
        on = o[p] * lax.rsqrt(jnp.mean(o[p] * o[p], axis=-1, keepdims=True) + EPS) * gn_ref[...]
        for hh in range(2):
            lo = (2 * p + hh) * dk
            zz = z_ref[:, lo:lo + dk]
            y = on[hh * c:(hh + 1) * c, :] * (zz * _sigmoid(zz))
            o_ref[:, lo:lo + dk] = y.astype(o_ref.dtype)


def gated_deltanet(proj_b, proj_small, conv_w, a_log, dt_bias, g_out, batch, seq, *, n_heads, dk,
                   small_col_block, small_row_array_cols, b_col, a_col, pairs):
    t_tok = batch * seq
    nchunk = seq // CHUNK
    width = pairs * 2 * dk
    ng = n_heads // (2 * pairs)
    hw = n_heads * dk
    sec = hw // width
    conv_width = conv_w.shape[0]
    alog_row = jnp.zeros((1, LANE), F32).at[0, a_col:a_col + n_heads].set(a_log.astype(F32))
    dtb_row = jnp.zeros((1, LANE), F32).at[0, a_col:a_col + n_heads].set(dt_bias.astype(F32))
    kern = functools.partial(_gdn_kernel, pairs=pairs, dk=dk, b_col=b_col, a_col=a_col, conv_width=conv_width)
    row = lambda b, g, n: b * nchunk + n
    return pl.pallas_call(
        kern,
        out_shape=jax.ShapeDtypeStruct((t_tok, hw), BF16),
        grid=(batch, ng, nchunk),
        in_specs=[
            pl.BlockSpec((CHUNK, width), lambda b, g, n: (row(b, g, n), g)),
            pl.BlockSpec((CHUNK, width), lambda b, g, n: (row(b, g, n), sec + g)),
            pl.BlockSpec((CHUNK, width), lambda b, g, n: (row(b, g, n), 2 * sec + g)),
            pl.BlockSpec((CHUNK, width), lambda b, g, n: (row(b, g, n), 3 * sec + g)),
            pl.BlockSpec((CHUNK, LANE), lambda b, g, n: (row(b, g, n), small_col_block)),
            pl.BlockSpec((conv_width, width), lambda b, g, n: (0, g)),
            pl.BlockSpec((conv_width, width), lambda b, g, n: (0, sec + g)),
            pl.BlockSpec((conv_width, width), lambda b, g, n: (0, 2 * sec + g)),
            pl.BlockSpec((1, LANE), lambda b, g, n: (0, 0)),
            pl.BlockSpec((1, LANE), lambda b, g, n: (0, 0)),
            pl.BlockSpec((1, dk), lambda b, g, n: (0, 0)),
        ],
        out_specs=pl.BlockSpec((CHUNK, width), lambda b, g, n: (row(b, g, n), g)),
        scratch_shapes=[pltpu.VMEM((3, CHUNK + 8, width), F32),
                        pltpu.VMEM((pairs, dk, 2 * dk), F32)],
        compiler_params=_cparams(("arbitrary", "arbitrary", "arbitrary")),
        name="gated_deltanet",
    )(proj_b, proj_b, proj_b, proj_b, proj_small, conv_w, conv_w, conv_w, alog_row, dtb_row,
      g_out.reshape(1, dk).astype(F32))


def _cross_kernel(h_ref, gq_ref, wcq_ref, k_ref, v_ref, wco_ref, g_ref, h_out_ref, n_out_ref, *, n_heads, dh):
    scale = dh ** -0.5
    h_in = h_ref[...]
    ms = jnp.mean(h_in * h_in, axis=-1, keepdims=True)
    hq = (h_in * lax.rsqrt(ms + EPS) * gq_ref[...]).astype(BF16)
    q = _dot(hq, wcq_ref[...]).astype(BF16)
    outs = []
    for h in range(n_heads):
        s = _dot_nt(q[:, h * dh:(h + 1) * dh], k_ref[:, h * dh:(h + 1) * dh]) * scale
        m = jnp.max(s, axis=-1, keepdims=True)
        p = jnp.exp(s - m)
        l = jnp.sum(p, axis=-1, keepdims=True)
        p = (p / l).astype(BF16)
        outs.append(_dot(p, v_ref[:, h * dh:(h + 1) * dh]).astype(BF16))
    o = jnp.concatenate(outs, axis=1)
    hn = h_in + _dot(o, wco_ref[...])
    h_out_ref[...] = hn
    ms = jnp.mean(hn * hn, axis=-1, keepdims=True)
    n_out_ref[...] = (hn * lax.rsqrt(ms + EPS) * g_ref[...]).astype(n_out_ref.dtype)


def cross_attention(h, g_q, w_cq, kvx, w_co, g_next, batch, seq, n_mem, *, n_heads, dh, tq=256):
    t_tok, d = h.shape
    tq = min(tq, seq)
    nq = seq // tq
    hw = n_heads * dh
    kern = functools.partial(_cross_kernel, n_heads=n_heads, dh=dh)
    once = pl.Buffered(1)
    return pl.pallas_call(
        kern,
        out_shape=(jax.ShapeDtypeStruct((t_tok, d), F32), jax.ShapeDtypeStruct((t_tok, d), BF16)),
        grid=(batch, nq),
        in_specs=[
            pl.BlockSpec((tq, d), lambda b, i: (b * nq + i, 0)),
            pl.BlockSpec((1, d), lambda b, i: (0, 0)),
            pl.BlockSpec((d, hw), lambda b, i: (0, 0), pipeline_mode=once),
            pl.BlockSpec((n_mem, hw), lambda b, i: (b, 0)),
            pl.BlockSpec((n_mem, hw), lambda b, i: (b, 1)),
            pl.BlockSpec((hw, d), lambda b, i: (0, 0), pipeline_mode=once),
            pl.BlockSpec((1, d), lambda b, i: (0, 0)),
        ],
        out_specs=(pl.BlockSpec((tq, d), lambda b, i: (b * nq + i, 0)),
                   pl.BlockSpec((tq, d), lambda b, i: (b * nq + i, 0))),
        compiler_params=_cparams(("parallel", "parallel")),
        name="cross_attention",
    )(h, g_q.reshape(1, d).astype(F32), w_cq, kvx, kvx, w_co, g_next.reshape(1, d).astype(F32))


def _pad_cols(w, n):
    return jnp.pad(w, ((0, 0), (0, n - w.shape[1])))


def kernel(x, mem, attn_norm_g, w_in, qa_norm_g, w_qb, kv_norm_g, w_uk, w_uv, w_iq, conv_w, a_log, dt_bias,
           delta_norm_g, w_o, cross_norm_g, mem_norm_g, w_cq, w_ckv, w_co, ffn_norm_g, w_ffn_in, w_ffn_out,
           final_norm_g):
    batch, seq, d = x.shape
    n_mem = mem.shape[1]
    depth = w_in.shape[0]
    q_lora = qa_norm_g.shape[1]
    kv_lora = kv_norm_g.shape[1]
    h_a, _, dh_a = w_uk.shape[1:]
    dv_a = w_uv.shape[3]
    h_idx = w_iq.shape[2] // 128
    d_idx = w_iq.shape[2] // h_idx
    h_b = a_log.shape[1]
    dk_b = delta_norm_g.shape[1]
    h_x = w_cq.shape[2] // 128
    dh_x = w_cq.shape[2] // h_x
    d_ff = w_ffn_out.shape[1]
    t_tok = batch * seq
    hw_b = h_b * dk_b

    h = x.reshape(t_tok, d)
    memf = mem.reshape(batch * n_mem, d)
    for l in range(depth):
        win_t = jnp.swapaxes(w_in[l], 0, 1)
        o_qa, o_ckv, o_kidx, o_widx = 0, q_lora, q_lora + kv_lora, q_lora + kv_lora + d_idx
        o_qkv = o_widx + h_idx
        o_z = o_qkv + 3 * hw_b
        o_b = o_z + hw_b
        o_a = o_b + h_b
        if o_qkv % 8 == 0:
            w_b_t, w_b_row0 = win_t, o_qkv
        else:
            w_b_t, w_b_row0 = win_t[o_qkv:o_qkv + 4 * hw_b], 0
        small_t = jnp.concatenate([win_t[o_widx:o_widx + h_idx], win_t[o_b:o_b + h_b], win_t[o_a:o_a + h_b]], axis=0)
        small_t = jnp.pad(small_t, ((0, LANE - small_t.shape[0]), (0, 0)))
        w_a = jnp.swapaxes(jnp.concatenate([win_t[o_qa:o_widx], small_t], axis=0), 0, 1)
        na = w_a.shape[1]
        assert q_lora % kv_lora == 0 and (q_lora + kv_lora) % d_idx == 0 and d_idx == LANE
        kidx_cb = (q_lora + kv_lora) // d_idx
        small_cb = kidx_cb + 1
        w_q2 = jnp.concatenate([w_qb[l], w_iq[l]], axis=1).astype(BF16)
        w_uk_b = w_uk[l].astype(BF16)
        w_uv_b = w_uv[l].astype(BF16)
        w_o_b = w_o[l]
        w_cq_b = w_cq[l].astype(BF16)
        w_ckv_b = w_ckv[l].astype(BF16)
        w_co_b = w_co[l].astype(BF16)
        ffp = -(-d_ff // 1024) * 1024
        w_fi_b = w_ffn_in[l]
        w_fo_b = jnp.pad(w_ffn_out[l], ((0, ffp - d_ff), (0, 0))).astype(BF16)

        n0 = rmsnorm_cols(h, attn_norm_g[l], d, 0, BF16)
        tn_a = na // 3 if na % (3 * LANE) == 0 else na
        proj_a = matmul([(n0, 0, w_a, 0, d)], na, F32, tm=1024, tn=tn_a, name="in_proj_a")
        proj_b = matmul([(n0, 0, w_b_t, 0, d)], 4 * hw_b, F32, tm=1024, tn=512, name="in_proj_b",
                        b_nk_row0=w_b_row0)
        qan = rmsnorm_cols(proj_a, qa_norm_g[l], q_lora, 0, BF16)
        ckvn = rmsnorm_cols(proj_a, kv_norm_g[l], kv_lora, q_lora // kv_lora, BF16)
        qcat = matmul([(qan, 0, w_q2, 0, q_lora)], w_q2.shape[1], BF16, tm=1024, tn=1024, name="q_proj")
        y_a = dsa_attention(qcat, proj_a, ckvn, w_uk_b, w_uv_b, batch, seq, n_heads=h_a, n_idx_heads=h_idx,
                            d_idx=d_idx, dh=dh_a, dv=dv_a, kidx_col_block=kidx_cb, small_col_block=small_cb)
        y_b = gated_deltanet(proj_b, proj_a, conv_w[l], a_log[l], dt_bias[l], delta_norm_g[l], batch, seq,
                             n_heads=h_b, dk=dk_b, small_col_block=small_cb, small_row_array_cols=na,
                             b_col=h_idx, a_col=h_idx + h_b, pairs=min(8, h_b // 2))
        k_a = h_a * dv_a
        assert k_a == hw_b
        h = matmul([(y_a, 0, w_o_b, 0, k_a), (y_b, 0, w_o_b, 1, k_a)], d, F32, tm=1024, tn=512, residual=h,
                   name="out_proj")

        memn = rmsnorm_cols(memf, mem_norm_g[l], d, 0, BF16)
        kvx = matmul([(memn, 0, w_ckv_b, 0, d)], w_ckv_b.shape[1], BF16, tm=1024, tn=512, name="mem_kv_proj")
        h, n2 = cross_attention(h, cross_norm_g[l], w_cq_b, kvx, w_co_b, ffn_norm_g[l], batch, seq, n_mem,
                                n_heads=h_x, dh=dh_x)

        act = matmul_swiglu(n2, w_fi_b, d_ff, ffp, tm=2048, tn=256)
        h = matmul([(act, 0, w_fo_b, 0, ffp)], d, F32, tm=1024, tn=1024, tk=ffp // 4, residual=h, name="ffn_out")
    out = rmsnorm_cols(h, final_norm_g, d, 0, F32)
    return out.reshape(batch, seq, d)
```

```python
import functools

import jax
import jax.numpy as jnp
from jax import lax
from jax.experimental import pallas as pl
from jax.experimental.pallas import tpu as pltpu

F32 = jnp.float32
BF16 = jnp.bfloat16
I32 = jnp.int32

EPS = 1e-6
CHUNK = 64
Q_BLOCK = 128
TOPK_MAX = 256
LANE = 128
VMEM_LIMIT = 56 * 1024 * 1024
INT_MIN = -2 ** 31
CHUNK_SHIFT = CHUNK.bit_length() - 1
assert 1 << CHUNK_SHIFT == CHUNK
NEG_BIG = -1e30


def _cparams(sem):
    return pltpu.CompilerParams(dimension_semantics=sem, vmem_limit_bytes=VMEM_LIMIT)


def _dot(a, b):
    return jnp.dot(a, b, preferred_element_type=F32)


def _dot_nt(a, b):
    return lax.dot_general(a, b, (((1,), (1,)), ((), ())), preferred_element_type=F32)


def _sigmoid(x):
    return 1.0 / (1.0 + jnp.exp(-x))


def _rmsnorm_kernel(x_ref, g_ref, o_ref):
    x = x_ref[...].astype(F32)
    ms = jnp.mean(x * x, axis=-1, keepdims=True)
    o_ref[...] = (x * lax.rsqrt(ms + EPS) * g_ref[...]).astype(o_ref.dtype)


def rmsnorm_cols(x, g, width, col_block, out_dtype, tr=256):
    m = x.shape[0]
    tr = min(tr, m)
    return pl.pallas_call(
        _rmsnorm_kernel,
        out_shape=jax.ShapeDtypeStruct((m, width), out_dtype),
        grid=(m // tr,),
        in_specs=[pl.BlockSpec((tr, width), lambda i: (i, col_block)),
                  pl.BlockSpec((1, width), lambda i: (0, 0))],
        out_specs=pl.BlockSpec((tr, width), lambda i: (i, 0)),
        compiler_params=_cparams(("parallel",)),
        name="rmsnorm",
    )(x, g.reshape(1, width).astype(F32))


def _rmsnorm2_kernel(xa_ref, xb_ref, ga_ref, gb_ref, oa_ref, ob_ref):
    for x_ref, g_ref, o_ref in ((xa_ref, ga_ref, oa_ref), (xb_ref, gb_ref, ob_ref)):
        x = x_ref[...].astype(F32)
        ms = jnp.mean(x * x, axis=-1, keepdims=True)
        o_ref[...] = (x * lax.rsqrt(ms + EPS) * g_ref[...]).astype(o_ref.dtype)


def rmsnorm_two_windows(x, ga, gb, wa, wb, out_dtype, tr=256):
    m = x.shape[0]
    tr = min(tr, m)
    assert wa % wb == 0
    return pl.pallas_call(
        _rmsnorm2_kernel,
        out_shape=(jax.ShapeDtypeStruct((m, wa), out_dtype), jax.ShapeDtypeStruct((m, wb), out_dtype)),
        grid=(m // tr,),
        in_specs=[pl.BlockSpec((tr, wa), lambda i: (i, 0)),
                  pl.BlockSpec((tr, wb), lambda i: (i, wa // wb)),
                  pl.BlockSpec((1, wa), lambda i: (0, 0)),
                  pl.BlockSpec((1, wb), lambda i: (0, 0))],
        out_specs=(pl.BlockSpec((tr, wa), lambda i: (i, 0)), pl.BlockSpec((tr, wb), lambda i: (i, 0))),
        compiler_params=_cparams(("parallel",)),
        name="rmsnorm_latents",
    )(x, x, ga.reshape(1, wa).astype(F32), gb.reshape(1, wb).astype(F32))


def _cast_pad_kernel(x_ref, o_ref, *, n_valid):
    i = pl.program_id(0)

    @pl.when(i < n_valid)
    def _():
        o_ref[...] = x_ref[...].astype(o_ref.dtype)

    @pl.when(i >= n_valid)
    def _():
        o_ref[...] = jnp.zeros(o_ref.shape, o_ref.dtype)


def cast_pad_rows(w, rows_out, tr=256):
    r, c = w.shape
    assert r % tr == 0 and rows_out % tr == 0
    n_valid = r // tr
    return pl.pallas_call(
        functools.partial(_cast_pad_kernel, n_valid=n_valid),
        out_shape=jax.ShapeDtypeStruct((rows_out, c), BF16),
        grid=(rows_out // tr,),
        in_specs=[pl.BlockSpec((tr, c), lambda i: (jnp.minimum(i, n_valid - 1), 0))],
        out_specs=pl.BlockSpec((tr, c), lambda i: (i, 0)),
        compiler_params=_cparams(("parallel",)),
        name="cast_pad_rows",
    )(w)


def _mm_kernel(*refs, n_pairs, nk, has_res, b_is_nk):
    a_refs = refs[0:2 * n_pairs:2]
    b_refs = refs[1:2 * n_pairs:2]
    pos = 2 * n_pairs
    r_ref = refs[pos] if has_res else None
    pos += int(has_res)
    o_ref = refs[pos]
    acc_ref = refs[pos + 1] if nk > 1 else None

    dot = _dot_nt if b_is_nk else _dot
    part = dot(a_refs[0][...], b_refs[0][...].astype(BF16))
    for a_ref, b_ref in zip(a_refs[1:], b_refs[1:]):
        part = part + dot(a_ref[...], b_ref[...].astype(BF16))

    def finish(acc):
        if has_res:
            acc = acc + r_ref[...]
        o_ref[...] = acc.astype(o_ref.dtype)

    if nk == 1:
        finish(part)
    else:
        k = pl.program_id(2)

        @pl.when(k == 0)
        def _():
            acc_ref[...] = part

        @pl.when(k > 0)
        def _():
            acc_ref[...] += part

        @pl.when(k == nk - 1)
        def _():
            finish(acc_ref[...])


def matmul(pairs, n, out_dtype, tm, tn, tk=None, residual=None, name="matmul", b_nk_row0=None):
    m = pairs[0][0].shape[0]
    tm = min(tm, m)
    tn = min(tn, n)
    kdim = pairs[0][4]
    tk = kdim if tk is None else min(tk, kdim)
    nk = kdim // tk
    assert m % tm == 0 and n % tn == 0 and kdim % tk == 0
    in_specs, args = [], []
    for a, acb, b, brb, kd in pairs:
        assert kd == kdim
        in_specs.append(pl.BlockSpec((tm, tk), functools.partial(lambda i, j, k, o: (i, o + k), o=acb * nk)))
        if b_nk_row0 is None:
            in_specs.append(pl.BlockSpec((tk, tn), functools.partial(lambda i, j, k, o: (o + k, j), o=brb * nk)))
        else:
            in_specs.append(pl.BlockSpec(
                (pl.Element(tn), pl.Element(tk)),
                functools.partial(lambda i, j, k, o: ((b_nk_row0 // 8 + j * (tn // 8)) * 8, (o + k) * tk),
                                  o=brb * nk)))
        args += [a, b]
    if residual is not None:
        in_specs.append(pl.BlockSpec((tm, tn), lambda i, j, k: (i, j)))
        args.append(residual)
    return pl.pallas_call(
        functools.partial(_mm_kernel, n_pairs=len(pairs), nk=nk, has_res=residual is not None,
                          b_is_nk=b_nk_row0 is not None),
        out_shape=jax.ShapeDtypeStruct((m, n), out_dtype),
        grid=(m // tm, n // tn, nk),
        in_specs=in_specs,
        out_specs=pl.BlockSpec((tm, tn), lambda i, j, k: (i, j)),
        scratch_shapes=[pltpu.VMEM((tm, tn), F32)] if nk > 1 else [],
        compiler_params=_cparams(("parallel", "parallel", "arbitrary")),
        name=name,
    )(*args)


def _swiglu_kernel(a_ref, bg_ref, bu_ref, o_ref, *, nj):
    j = pl.program_id(1)

    @pl.when(j < nj)
    def _():
        bg = bg_ref[...].astype(BF16)
        bu = bu_ref[...].astype(BF16)
        tm = a_ref.shape[0]
        rc = 512 if tm % 512 == 0 else tm
        for r0 in range(0, tm, rc):
            a = a_ref[r0:r0 + rc, :]
            gate = _dot(a, bg)
            up = _dot(a, bu)
            o_ref[r0:r0 + rc, :] = (gate * _sigmoid(gate) * up).astype(o_ref.dtype)

    @pl.when(j >= nj)
    def _():
        o_ref[...] = jnp.zeros(o_ref.shape, o_ref.dtype)


def matmul_swiglu(a, w, d_ff, d_out, tm, tn):
    m, k = a.shape
    tm = min(tm, m)
    tn = min(tn, d_ff)
    assert d_ff % tn == 0 and d_out % tn == 0
    nj = d_ff // tn
    return pl.pallas_call(
        functools.partial(_swiglu_kernel, nj=nj),
        out_shape=jax.ShapeDtypeStruct((m, d_out), BF16),
        grid=(m // tm, d_out // tn),
        in_specs=[pl.BlockSpec((tm, k), lambda i, j: (i, 0), pipeline_mode=pl.Buffered(1)),
                  pl.BlockSpec((k, tn), lambda i, j: (0, jnp.minimum(j, nj - 1))),
                  pl.BlockSpec((k, tn), lambda i, j: (0, jnp.minimum(j, nj - 1) + nj))],
        out_specs=pl.BlockSpec((tm, tn), lambda i, j: (i, j)),
        compiler_params=_cparams(("parallel", "parallel")),
        name="ffn_in_swiglu",
    )(a, w, w)


def _dsa_kernel(q_ref, qi_ref, sm_ref, kidx_ref, ckv_ref, wuk_ref, wuv_ref, o_ref,
                key_ref, bias_ref, st_ref, s_ref, p_ref, pv_ref, acc_ref, ql_ref, qis_ref, wb_ref, m_ref, l_ref, al_ref,
                *, n_heads, n_idx_heads, d_idx, dh, dv, tk, topk):
    qb = q_ref.shape[0]
    i = pl.program_id(1)
    n_keys = (i + 1) * qb
    nkt = (n_keys + tk - 1) // tk
    idx_scale = d_idx ** -0.5 * n_idx_heads ** -0.5
    att_scale = dh ** -0.5
    nlc = tk // LANE

    assert qb == LANE
    w_t = sm_ref[...].T * idx_scale
    for h in range(n_idx_heads):
        wb_ref[h] = jnp.broadcast_to(w_t[h:h + 1, :], (LANE, qb))
        qis_ref[h * qb:(h + 1) * qb, :] = qi_ref[:, h * d_idx:(h + 1) * d_idx]

    q_lane = lax.broadcasted_iota(I32, (LANE, qb), 1)
    k_row = lax.broadcasted_iota(I32, (LANE, qb), 0)
    lim = (((i * qb + q_lane) >> CHUNK_SHIFT) + 1) << CHUNK_SHIFT

    def score_tile(t, carry):
        ks = pl.multiple_of(t * tk, tk)
        kblk = kidx_ref[pl.ds(ks, tk), :].astype(BF16)
        st_ref[...] = _dot_nt(kblk, qis_ref[...])
        for c in range(nlc):
            sc = jnp.zeros((LANE, qb), F32)
            for h in range(n_idx_heads):
                sc = sc + jnp.maximum(st_ref[c * LANE:(c + 1) * LANE, h * qb:(h + 1) * qb], 0.0) * wb_ref[h]
            bits = pltpu.bitcast(sc, I32)
            key = bits ^ ((bits >> 31) & 0x7FFFFFFF)
            key = jnp.where(ks + c * LANE + k_row < lim, key, INT_MIN)
            key_ref[pl.ds(ks + c * LANE, LANE), :] = key
        return carry

    lax.fori_loop(0, nkt, score_tile, 0)

    pos_tile = lax.broadcasted_iota(I32, (tk, qb), 0)
    acc_rows = min(64, tk)

    def count(pred):
        def body(t, c):
            ks = pl.multiple_of(t * tk, tk)
            hit = jnp.where(pred(key_ref[pl.ds(ks, tk), :], ks + pos_tile), 1.0, 0.0)
            return c + jnp.sum(hit.reshape(tk // acc_rows, acc_rows, qb), axis=0)
        c = lax.fori_loop(0, nkt, body, jnp.zeros((acc_rows, qb), F32))
        return jnp.sum(c, axis=0, keepdims=True)

    kf = float(topk)
    t0 = jnp.where(count(lambda k, p: k >= 0) >= kf, 0, INT_MIN).astype(I32)

    def bisect(j, t):
        cand = t + (jnp.int32(1) << (30 - j))
        return jnp.where(count(lambda k, p: k >= cand) >= kf, cand, t)

    thr = lax.fori_loop(0, 31, bisect, t0)
    thr = jnp.maximum(thr, INT_MIN + 1)

    n_ge = count(lambda k, p: k >= thr)

    def tie_search():
        need = kf - count(lambda k, p: k > thr)

        def step(j, lo):
            cand = lo + (jnp.int32(1) << (pos_bits - 1 - j))
            below = count(lambda k, p: (k == thr) & (p < cand))
            return jnp.where(below < need, cand, lo)

        return lax.fori_loop(0, pos_bits, step, jnp.zeros((1, qb), I32))

    pos_bits = max(1, (key_ref.shape[0] - 1).bit_length())
    last = lax.cond(jnp.max(n_ge) > kf, tie_search, lambda: jnp.full((1, qb), key_ref.shape[0], I32))

    def bias_tile(t, carry):
        ks = pl.multiple_of(t * tk, tk)
        for c in range(nlc):
            k = key_ref[pl.ds(ks + c * LANE, LANE), :]
            tied = jnp.where(ks + c * LANE + k_row <= last, 0.0, NEG_BIG)
            b = jnp.where(k > thr, 0.0, jnp.where(k == thr, tied, NEG_BIG))
            bias_ref[:, pl.ds(ks + c * LANE, LANE)] = b.T
        return carry

    lax.fori_loop(0, nkt, bias_tile, 0)

    rows = n_heads * qb
    for h in range(n_heads):
        ql = _dot_nt(q_ref[:, h * dh:(h + 1) * dh], wuk_ref[h]) * att_scale
        ql_ref[h * qb:(h + 1) * qb, :] = ql.astype(BF16)
    m_ref[...] = jnp.full((rows, LANE), NEG_BIG, F32)
    l_ref[...] = jnp.zeros((rows, LANE), F32)
    acc_ref[...] = jnp.zeros(acc_ref.shape, F32)

    hpg = 4 if n_heads % 4 == 0 else 1
    grows = hpg * qb

    def attend(t, carry):
        ks = pl.multiple_of(t * tk, tk)
        ckv_t = ckv_ref[pl.ds(ks, tk), :]
        for g in range(n_heads // hpg):
            g0 = g * grows
            s_ref[g0:g0 + grows, :] = _dot_nt(ql_ref[g0:g0 + grows, :], ckv_t)
        for g in range(n_heads // hpg):
            g0 = g * grows
            for h in range(g * hpg, (g + 1) * hpg):
                r0 = h * qb

                def masked(c):
                    return s_ref[r0:r0 + qb, c * LANE:(c + 1) * LANE] + bias_ref[:, pl.ds(ks + c * LANE, LANE)]

                tmax = masked(0)
                for c in range(1, nlc):
                    tmax = jnp.maximum(tmax, masked(c))
                m_old = m_ref[r0:r0 + qb, :]
                m_new = jnp.maximum(m_old, jnp.max(tmax, axis=-1, keepdims=True))
                alpha = jnp.exp(m_old - m_new)
                lsum = l_ref[r0:r0 + qb, :] * alpha
                for c in range(nlc):
                    p = jnp.exp(masked(c) - m_new)
                    lsum = lsum + p
                    p_ref[r0:r0 + qb, c * LANE:(c + 1) * LANE] = p.astype(BF16)
                m_ref[r0:r0 + qb, :] = m_new
                l_ref[r0:r0 + qb, :] = lsum
                al_ref[r0:r0 + qb, :] = alpha
            pv_ref[g0:g0 + grows, :] = _dot(p_ref[g0:g0 + grows, :], ckv_t)
            for h in range(g * hpg, (g + 1) * hpg):
                r0 = h * qb
                alpha = al_ref[r0:r0 + qb, :]
                for c in range(ckv_t.shape[1] // LANE):
                    cs = slice(c * LANE, (c + 1) * LANE)
                    acc_ref[r0:r0 + qb, cs] = acc_ref[r0:r0 + qb, cs] * alpha + pv_ref[r0:r0 + qb, cs]
        return carry

    lax.fori_loop(0, nkt, attend, 0)
    for h in range(n_heads):
        r0 = h * qb
        lrow = jnp.sum(l_ref[r0:r0 + qb, :], axis=-1, keepdims=True)
        out = _dot(acc_ref[r0:r0 + qb, :].astype(BF16), wuv_ref[h]) / lrow
        o_ref[:, h * dv:(h + 1) * dv] = out.astype(o_ref.dtype)


def dsa_attention(qcat, proj_a, ckvn, w_uk, w_uv, batch, seq, *, n_heads, n_idx_heads, d_idx, dh, dv,
                  kidx_col_block, small_col_block):
    t_tok = batch * seq
    kv_lora = ckvn.shape[1]
    qb = min(Q_BLOCK, seq)
    nb = seq // qb
    tk = min(512, seq)
    topk = min(TOPK_MAX, seq // 4)
    qw = n_heads * dh
    qiw = n_idx_heads * d_idx
    assert qw == qiw and kv_lora <= tk
    kern = functools.partial(_dsa_kernel, n_heads=n_heads, n_idx_heads=n_idx_heads, d_idx=d_idx, dh=dh, dv=dv,
                             tk=tk, topk=topk)
    once = pl.Buffered(1)
    return pl.pallas_call(
        kern,
        out_shape=jax.ShapeDtypeStruct((t_tok, n_heads * dv), BF16),
        grid=(batch, nb),
        in_specs=[
            pl.BlockSpec((qb, qw), lambda b, i: (b * nb + i, 0)),
            pl.BlockSpec((qb, qiw), lambda b, i: (b * nb + i, 1)),
            pl.BlockSpec((qb, LANE), lambda b, i: (b * nb + i, small_col_block)),
            pl.BlockSpec((seq, d_idx), lambda b, i: (b, kidx_col_block), pipeline_mode=once),
            pl.BlockSpec((seq, kv_lora), lambda b, i: (b, 0), pipeline_mode=once),
            pl.BlockSpec(w_uk.shape, lambda b, i: (0, 0, 0), pipeline_mode=once),
            pl.BlockSpec(w_uv.shape, lambda b, i: (0, 0, 0), pipeline_mode=once),
        ],
        out_specs=pl.BlockSpec((qb, n_heads * dv), lambda b, i: (b * nb + i, 0)),
        scratch_shapes=[
            pltpu.VMEM((seq, qb), I32),
            pltpu.VMEM((qb, seq), F32),
            pltpu.VMEM((tk, n_idx_heads * qb), F32),
            pltpu.VMEM((n_heads * qb, tk), F32),
            pltpu.VMEM((n_heads * qb, tk), BF16),
            pltpu.VMEM((n_heads * qb, kv_lora), F32),
            pltpu.VMEM((n_heads * qb, kv_lora), F32),
            pltpu.VMEM((n_heads * qb, kv_lora), BF16),
            pltpu.VMEM((n_idx_heads * qb, d_idx), BF16),
            pltpu.VMEM((n_idx_heads, LANE, qb), F32),
            pltpu.VMEM((n_heads * qb, LANE), F32),
            pltpu.VMEM((n_heads * qb, LANE), F32),
            pltpu.VMEM((n_heads * qb, LANE), F32),
        ],
        compiler_params=_cparams(("arbitrary", "arbitrary")),
        name="dsa_attention",
    )(qcat, qcat, proj_a, proj_a, ckvn, w_uk, w_uv)


def _split_bf16(x):
    hi = x.astype(BF16)
    lo = (x - hi.astype(F32)).astype(BF16)
    return hi, lo


def _dot3(a, b):
    ah, al = _split_bf16(a)
    bh, bl = _split_bf16(b)
    lhs = jnp.concatenate([ah, ah, al], axis=1)
    rhs = jnp.concatenate([bh, bl, bh], axis=0)
    return _dot(lhs, rhs)


def _cumsum_rows(x):
    n = x.shape[0]
    row = lax.broadcasted_iota(I32, x.shape, 0)
    s = 1
    while s < n:
        x = x + jnp.where(row >= s, pltpu.roll(x, s, 0), 0.0)
        s *= 2
    return x


def _gdn_kernel(xq_ref, xk_ref, xv_ref, z_ref, gate_ref, cwq_ref, cwk_ref, cwv_ref, alog_ref, dtb_ref, gn_ref,
                o_ref, ext_ref, state_ref, *, pairs, dk, b_col, a_col, conv_width):
    c = CHUNK
    assert dk == 2 * c
    n = pl.program_id(2)
    hg = pl.program_id(1)
    width = pairs * 2 * dk
    halo = 8

    @pl.when(n == 0)
    def _():
        ext_ref[:, 0:halo, :] = jnp.zeros((3, halo, width), F32)
        state_ref[...] = jnp.zeros(state_ref.shape, F32)

    prs = range(pairs)

    def conv_silu(idx, cw_ref, p):
        halves = []
        for hh in range(2):
            lo = (2 * p + hh) * dk
            win = ext_ref[idx, :, lo:lo + dk]
            y = win[halo:, :] * cw_ref[conv_width - 1:conv_width, lo:lo + dk]
            for j in range(conv_width - 1):
                back = conv_width - 1 - j
                y = y + pltpu.roll(win, back, 0)[halo:, :] * cw_ref[j:j + 1, lo:lo + dk]
            halves.append(y * _sigmoid(y))
        return jnp.concatenate(halves, axis=0)

    for idx, x_ref in enumerate((xq_ref, xk_ref, xv_ref)):
        ext_ref[idx, halo:halo + c, :] = x_ref[...]
    q = [conv_silu(0, cwq_ref, p) for p in prs]
    k = [conv_silu(1, cwk_ref, p) for p in prs]
    v = [conv_silu(2, cwv_ref, p) for p in prs]
    for idx in range(3):
        ext_ref[idx, 0:halo, :] = ext_ref[idx, c:c + halo, :]

    gates = gate_ref[...]
    beta_all = _sigmoid(gates)
    sp = jnp.maximum(gates + dtb_ref[...], 0.0) + jnp.log(1.0 + jnp.exp(-jnp.abs(gates + dtb_ref[...])))
    g_all = -jnp.exp(alog_ref[...]) * sp
    gcum_all = _cumsum_rows(g_all)

    r2 = lax.broadcasted_iota(I32, (2 * c, 2 * c), 0)
    c2 = lax.broadcasted_iota(I32, (2 * c, 2 * c), 1)
    same = (r2 ^ c2) < c
    incl = same & (r2 >= c2)
    strict = same & (r2 > c2)
    top = lax.broadcasted_iota(I32, (2 * c, dk), 0) < c

    eye = jnp.where(r2 == c2, 1.0, 0.0)
    lane = lax.broadcasted_iota(I32, (c, LANE), 1)

    def pick(x, base, p):
        h0 = (hg * pairs + p) * 2
        a = jnp.sum(jnp.where(lane == base + h0, x, 0.0), axis=-1, keepdims=True)
        b = jnp.sum(jnp.where(lane == base + h0 + 1, x, 0.0), axis=-1, keepdims=True)
        return jnp.concatenate([jnp.broadcast_to(a, (c, dk)), jnp.broadcast_to(b, (c, dk))], axis=0)

    def bdot(a, b):
        return _dot(a.astype(BF16), b.astype(BF16))

    q = [x * lax.rsqrt(jnp.sum(x * x, axis=-1, keepdims=True) + EPS) * (dk ** -0.5) for x in q]
    k = [x * lax.rsqrt(jnp.sum(x * x, axis=-1, keepdims=True) + EPS) for x in k]
    beta = [pick(beta_all, b_col, p) for p in prs]
    gc = [pick(gcum_all, a_col, p) for p in prs]
    glast = [jnp.concatenate([jnp.broadcast_to(g[c - 1:c, :], (c, dk)),
                              jnp.broadcast_to(g[2 * c - 1:2 * c, :], (c, dk))], axis=0) for g in gc]
    eg = [jnp.exp(g) for g in gc]
    decay = [jnp.exp(jnp.where(incl, g - g.T, -jnp.inf)) for g in gc]
    kb = [k[p] * beta[p] for p in prs]
    kbf = [x.astype(BF16) for x in k]
    kk = [_dot_nt(kb[p].astype(BF16), kbf[p]) for p in prs]
    qk = [_dot_nt(q[p].astype(BF16), kbf[p]) for p in prs]
    nmat = [jnp.where(strict, -(kk[p] * decay[p]), 0.0) for p in prs]

    tinv = [eye + nm for nm in nmat]
    pw = nmat
    for _ in range(5):
        pw = [bdot(x, x) for x in pw]
        tinv = [tinv[p] + bdot(tinv[p], pw[p]) for p in prs]

    rhs = [jnp.concatenate([v[p] * beta[p], kb[p] * eg[p]], axis=1) for p in prs]
    x1 = [bdot(tinv[p], rhs[p]) for p in prs]
    resid = [rhs[p] - x1[p] + _dot3(nmat[p], x1[p]) for p in prs]
    xs = [x1[p] + bdot(tinv[p], resid[p]) for p in prs]
    u = [x[:, 0:dk] for x in xs]
    w = [x[:, dk:2 * dk] for x in xs]

    amat = [jnp.where(incl, qk[p] * decay[p], 0.0) for p in prs]
    q_dec = [q[p] * eg[p] for p in prs]
    k_dec_t = [(k[p] * jnp.exp(glast[p] - gc[p])).T.astype(BF16) for p in prs]
    cd = [jnp.exp(g) for g in glast]

    s2 = [state_ref[p] for p in prs]
    s2b = [x.astype(BF16) for x in s2]
    ws = [_dot(w[p].astype(BF16), s2b[p]) for p in prs]
    qs = [_dot(q_dec[p].astype(BF16), s2b[p]) for p in prs]
    v_new = [u[p] - jnp.where(top, ws[p][:, 0:dk], ws[p][:, dk:2 * dk]) for p in prs]
    vb = [x.astype(BF16) for x in v_new]
    o = [jnp.where(top, qs[p][:, 0:dk], qs[p][:, dk:2 * dk]) + _dot(amat[p].astype(BF16), vb[p]) for p in prs]
    zero = jnp.zeros((2 * c, dk), BF16)
    vexp = [jnp.concatenate([jnp.where(top, x, zero), jnp.where(top, zero, x)], axis=1) for x in vb]
    upd = [_dot(k_dec_t[p], vexp[p]) for p in prs]
    for p in prs:
        cdrow = jnp.concatenate([cd[p][0:1, :], cd[p][c:c + 1, :]], axis=1)
        state_ref[p] = s2[p] * cdrow + upd[p]

    for p in prs:
        on = o[p] * lax.rsqrt(jnp.mean(o[p] * o[p], axis=-1, keepdims=True) + EPS) * gn_ref[...]
        for hh in range(2):
            lo = (2 * p + hh) * dk
            zz = z_ref[:, lo:lo + dk]
            y = on[hh * c:(hh + 1) * c, :] * (zz * _sigmoid(zz))
            o_ref[:, lo:lo + dk] = y.astype(o_ref.dtype)


def gated_deltanet(proj_b, proj_small, conv_w, a_log, dt_bias, g_out, batch, seq, *, n_heads, dk,
                   small_col_block, small_row_array_cols, b_col, a_col, pairs):
    t_tok = batch * seq
    nchunk = seq // CHUNK
    width = pairs * 2 * dk
    ng = n_heads // (2 * pairs)
    hw = n_heads * dk
    sec = hw // width
    conv_width = conv_w.shape[0]
    alog_row = jnp.zeros((1, LANE), F32).at[0, a_col:a_col + n_heads].set(a_log.astype(F32))
    dtb_row = jnp.zeros((1, LANE), F32).at[0, a_col:a_col + n_heads].set(dt_bias.astype(F32))
    kern = functools.partial(_gdn_kernel, pairs=pairs, dk=dk, b_col=b_col, a_col=a_col, conv_width=conv_width)
    row = lambda b, g, n: b * nchunk + n
    return pl.pallas_call(
        kern,
        out_shape=jax.ShapeDtypeStruct((t_tok, hw), BF16),
        grid=(batch, ng, nchunk),
        in_specs=[
            pl.BlockSpec((CHUNK, width), lambda b, g, n: (row(b, g, n), g)),
            pl.BlockSpec((CHUNK, width), lambda b, g, n: (row(b, g, n), sec + g)),
            pl.BlockSpec((CHUNK, width), lambda b, g, n: (row(b, g, n), 2 * sec + g)),
            pl.BlockSpec((CHUNK, width), lambda b, g, n: (row(b, g, n), 3 * sec + g)),
            pl.BlockSpec((CHUNK, LANE), lambda b, g, n: (row(b, g, n), small_col_block)),
            pl.BlockSpec((conv_width, width), lambda b, g, n: (0, g)),
            pl.BlockSpec((conv_width, width), lambda b, g, n: (0, sec + g)),
            pl.BlockSpec((conv_width, width), lambda b, g, n: (0, 2 * sec + g)),
            pl.BlockSpec((1, LANE), lambda b, g, n: (0, 0)),
            pl.BlockSpec((1, LANE), lambda b, g, n: (0, 0)),
            pl.BlockSpec((1, dk), lambda b, g, n: (0, 0)),
        ],
        out_specs=pl.BlockSpec((CHUNK, width), lambda b, g, n: (row(b, g, n), g)),
        scratch_shapes=[pltpu.VMEM((3, CHUNK + 8, width), F32),
                        pltpu.VMEM((pairs, dk, 2 * dk), F32)],
        compiler_params=_cparams(("arbitrary", "arbitrary", "arbitrary")),
        name="gated_deltanet",
    )(proj_b, proj_b, proj_b, proj_b, proj_small, conv_w, conv_w, conv_w, alog_row, dtb_row,
      g_out.reshape(1, dk).astype(F32))


def _cross_kernel(h_ref, gq_ref, wcq_ref, k_ref, v_ref, wco_ref, g_ref, h_out_ref, n_out_ref, *, n_heads, dh):
    scale = dh ** -0.5
    h_in = h_ref[...]
    ms = jnp.mean(h_in * h_in, axis=-1, keepdims=True)
    hq = (h_in * lax.rsqrt(ms + EPS) * gq_ref[...]).astype(BF16)
    q = _dot(hq, wcq_ref[...]).astype(BF16)
    outs = []
    for h in range(n_heads):
        s = _dot_nt(q[:, h * dh:(h + 1) * dh], k_ref[:, h * dh:(h + 1) * dh]) * scale
        m = jnp.max(s, axis=-1, keepdims=True)
        p = jnp.exp(s - m)
        l = jnp.sum(p, axis=-1, keepdims=True)
        p = (p / l).astype(BF16)
        outs.append(_dot(p, v_ref[:, h * dh:(h + 1) * dh]).astype(BF16))
    o = jnp.concatenate(outs, axis=1)
    hn = h_in + _dot(o, wco_ref[...])
    h_out_ref[...] = hn
    ms = jnp.mean(hn * hn, axis=-1, keepdims=True)
    n_out_ref[...] = (hn * lax.rsqrt(ms + EPS) * g_ref[...]).astype(n_out_ref.dtype)


def cross_attention(h, g_q, w_cq, kvx, w_co, g_next, batch, seq, n_mem, *, n_heads, dh, tq=256):
    t_tok, d = h.shape
    tq = min(tq, seq)
    nq = seq // tq
    hw = n_heads * dh
    kern = functools.partial(_cross_kernel, n_heads=n_heads, dh=dh)
    once = pl.Buffered(1)
    return pl.pallas_call(
        kern,
        out_shape=(jax.ShapeDtypeStruct((t_tok, d), F32), jax.ShapeDtypeStruct((t_tok, d), BF16)),
        grid=(batch, nq),
        in_specs=[
            pl.BlockSpec((tq, d), lambda b, i: (b * nq + i, 0)),
            pl.BlockSpec((1, d), lambda b, i: (0, 0)),
            pl.BlockSpec((d, hw), lambda b, i: (0, 0), pipeline_mode=once),
            pl.BlockSpec((n_mem, hw), lambda b, i: (b, 0)),
            pl.BlockSpec((n_mem, hw), lambda b, i: (b, 1)),
            pl.BlockSpec((hw, d), lambda b, i: (0, 0), pipeline_mode=once),
            pl.BlockSpec((1, d), lambda b, i: (0, 0)),
        ],
        out_specs=(pl.BlockSpec((tq, d), lambda b, i: (b * nq + i, 0)),
                   pl.BlockSpec((tq, d), lambda b, i: (b * nq + i, 0))),
        compiler_params=_cparams(("parallel", "parallel")),
        name="cross_attention",
    )(h, g_q.reshape(1, d).astype(F32), w_cq, kvx, kvx, w_co, g_next.reshape(1, d).astype(F32))


def _pad_cols(w, n):
    return jnp.pad(w, ((0, 0), (0, n - w.shape[1])))


def kernel(x, mem, attn_norm_g, w_in, qa_norm_g, w_qb, kv_norm_g, w_uk, w_uv, w_iq, conv_w, a_log, dt_bias,
           delta_norm_g, w_o, cross_norm_g, mem_norm_g, w_cq, w_ckv, w_co, ffn_norm_g, w_ffn_in, w_ffn_out,
           final_norm_g):
    batch, seq, d = x.shape
    n_mem = mem.shape[1]
    depth = w_in.shape[0]
    q_lora = qa_norm_g.shape[1]
    kv_lora = kv_norm_g.shape[1]
    h_a, _, dh_a = w_uk.shape[1:]
    dv_a = w_uv.shape[3]
    h_idx = w_iq.shape[2] // 128
    d_idx = w_iq.shape[2] // h_idx
    h_b = a_log.shape[1]
    dk_b = delta_norm_g.shape[1]
    h_x = w_cq.shape[2] // 128
    dh_x = w_cq.shape[2] // h_x
    d_ff = w_ffn_out.shape[1]
    t_tok = batch * seq
    hw_b = h_b * dk_b

    h = x.reshape(t_tok, d)
    memf = mem.reshape(batch * n_mem, d)
    for l in range(depth):
        win_t = jnp.swapaxes(w_in[l], 0, 1)
        o_qa, o_ckv, o_kidx, o_widx = 0, q_lora, q_lora + kv_lora, q_lora + kv_lora + d_idx
        o_qkv = o_widx + h_idx
        o_z = o_qkv + 3 * hw_b
        o_b = o_z + hw_b
        o_a = o_b + h_b
        if o_qkv % 8 == 0:
            w_b_t, w_b_row0 = win_t, o_qkv
        else:
            w_b_t, w_b_row0 = win_t[o_qkv:o_qkv + 4 * hw_b], 0
        small_t = jnp.concatenate([win_t[o_widx:o_widx + h_idx], win_t[o_b:o_b + h_b], win_t[o_a:o_a + h_b]], axis=0)
        small_t = jnp.pad(small_t, ((0, LANE - small_t.shape[0]), (0, 0)))
        w_a = jnp.swapaxes(jnp.concatenate([win_t[o_qa:o_widx], small_t], axis=0), 0, 1)
        na = w_a.shape[1]
        assert q_lora % kv_lora == 0 and (q_lora + kv_lora) % d_idx == 0 and d_idx == LANE
        kidx_cb = (q_lora + kv_lora) // d_idx
        small_cb = kidx_cb + 1
        w_q2 = jnp.concatenate([w_qb[l], w_iq[l]], axis=1).astype(BF16)
        w_uk_b = w_uk[l].astype(BF16)
        w_uv_b = w_uv[l].astype(BF16)
        w_o_b = w_o[l]
        w_cq_b = w_cq[l].astype(BF16)
        w_ckv_b = w_ckv[l].astype(BF16)
        w_co_b = w_co[l].astype(BF16)
        ffp = -(-d_ff // 1024) * 1024
        w_fi_b = w_ffn_in[l]
        w_fo_b = cast_pad_rows(w_ffn_out[l], ffp)

        n0 = rmsnorm_cols(h, attn_norm_g[l], d, 0, BF16)
        tn_a = na // 3 if na % (3 * LANE) == 0 else na
        proj_a = matmul([(n0, 0, w_a, 0, d)], na, F32, tm=1024, tn=tn_a, name="in_proj_a")
        proj_b = matmul([(n0, 0, w_b_t, 0, d)], 4 * hw_b, F32, tm=1024, tn=512, name="in_proj_b",
                        b_nk_row0=w_b_row0)
        qan, ckvn = rmsnorm_two_windows(proj_a, qa_norm_g[l], kv_norm_g[l], q_lora, kv_lora, BF16)
        qcat = matmul([(qan, 0, w_q2, 0, q_lora)], w_q2.shape[1], BF16, tm=1024, tn=1024, name="q_proj")
        y_a = dsa_attention(qcat, proj_a, ckvn, w_uk_b, w_uv_b, batch, seq, n_heads=h_a, n_idx_heads=h_idx,
                            d_idx=d_idx, dh=dh_a, dv=dv_a, kidx_col_block=kidx_cb, small_col_block=small_cb)
        y_b = gated_deltanet(proj_b, proj_a, conv_w[l], a_log[l], dt_bias[l], delta_norm_g[l], batch, seq,
                             n_heads=h_b, dk=dk_b, small_col_block=small_cb, small_row_array_cols=na,
                             b_col=h_idx, a_col=h_idx + h_b, pairs=min(8, h_b // 2))
        k_a = h_a * dv_a
        assert k_a == hw_b
        h = matmul([(y_a, 0, w_o_b, 0, k_a), (y_b, 0, w_o_b, 1, k_a)], d, F32, tm=1024, tn=512, residual=h,
                   name="out_proj")

        memn = rmsnorm_cols(memf, mem_norm_g[l], d, 0, BF16)
        kvx = matmul([(memn, 0, w_ckv_b, 0, d)], w_ckv_b.shape[1], BF16, tm=1024, tn=512, name="mem_kv_proj")
        h, n2 = cross_attention(h, cross_norm_g[l], w_cq_b, kvx, w_co_b, ffn_norm_g[l], batch, seq, n_mem,
                                n_heads=h_x, dh=dh_x)

        act = matmul_swiglu(n2, w_fi_b, d_ff, ffp, tm=2048, tn=256)
        h = matmul([(act, 0, w_fo_b, 0, ffp)], d, F32, tm=1024, tn=1024, tk=ffp // 4, residual=h, name="ffn_out")
    out = rmsnorm_cols(h, final_norm_g, d, 0, F32)
    return out.reshape(batch, seq, d)
```
